```python
import math
import jax, jax.numpy as jnp
from jax import lax
import numpy as np

D_MODEL = 2048
BATCH = 2
SEQ = 8192
DEPTH = 1

HEAD_DIM = 128
N_MIX_HEADS = D_MODEL // HEAD_DIM
NSA_HEADS = N_MIX_HEADS // 2
NSA_KV_GROUPS = 2
NSA_HPG = NSA_HEADS // NSA_KV_GROUPS
NSA_CMP_STRIDE = 16
NSA_CMP_LEN = 2 * NSA_CMP_STRIDE
NSA_SEL_BLOCK = 64
NSA_TOPN = 16
NSA_WINDOW = 512
NSA_FORCE_BONUS = 1.0e4
MOBA_HEADS = N_MIX_HEADS - NSA_HEADS
MOBA_BLOCK = 256
MOBA_TOPK = 3
Q_BLK = 64
MEM_LEN = 256
XA_HEADS = 4
XA_DIM = 128
PEER_HEADS = 8
PEER_NKEYS = 128
PEER_NEXPERTS = PEER_NKEYS * PEER_NKEYS
PEER_QDIM = 256
PEER_TOPK = 16
PEER_TOK_BLK = 128
DN_ALPHA = (2 * DEPTH) ** 0.25
DN_BETA = (8 * DEPTH) ** -0.25
LN_EPS = 1e-5

NSA_Q_COLS = NSA_HEADS * HEAD_DIM
NSA_KV_COLS = NSA_KV_GROUPS * HEAD_DIM
NSA_GATE_COLS = NSA_HEADS * 3
MOBA_COLS = MOBA_HEADS * HEAD_DIM
IN_SPLITS = [NSA_Q_COLS] + [NSA_KV_COLS] * 6 + [NSA_GATE_COLS] + [MOBA_COLS] * 3
IN_COLS = sum(IN_SPLITS)

kernel_name = 'nsa_moba_peer_deepnorm_hybrid'


def _alibi_slopes():
    s = 2.0 ** (-8.0 * (np.arange(N_MIX_HEADS) + 1) / N_MIX_HEADS)
    s = s.astype(np.float32)
    return (jnp.asarray(s[0::2]).reshape(NSA_KV_GROUPS, NSA_HPG), jnp.asarray(s[1::2]))


def _layernorm(x, g, b):
    xf = x.astype(jnp.float32)
    mu = xf.mean(-1, keepdims=True)
    var = jnp.square(xf - mu).mean(-1, keepdims=True)
    return ((xf - mu) * lax.rsqrt(var + LN_EPS) * g.astype(jnp.float32) + b.astype(jnp.float32)).astype(x.dtype)


def _masked_softmax(s, mask):
    s = jnp.where(mask, s, -jnp.inf)
    m = jnp.max(s, -1, keepdims=True)
    m = jnp.where(jnp.isfinite(m), m, 0.0)
    e = jnp.where(mask, jnp.exp(s - m), 0.0)
    d = e.sum(-1, keepdims=True)
    return e / jnp.where(d > 0, d, 1.0)


def _gather_blocks(kb, idx):
    return jax.vmap(jax.vmap(lambda t, i: t[i]))(kb, idx)


def _compress(raw, pe, w1, w2):
    B, S, G, dk = raw.shape
    c = raw.reshape(B, S // NSA_CMP_STRIDE, NSA_CMP_STRIDE, G, dk)
    blocks = jnp.concatenate([c[:, :-1], c[:, 1:]], axis=2) + pe[:, None, :]
    hid = jax.nn.gelu(jnp.einsum('bnlgd,lde->bnge', blocks, w1))
    return jnp.einsum('bnge,ef->bgnf', hid, w2)


def _cmp_to_sel(p, n_sel):
    r = NSA_SEL_BLOCK // NSA_CMP_STRIDE
    nc = p.shape[-1]
    total = r * n_sel + 1
    pp = jnp.pad(p, [(0, 0)] * (p.ndim - 1) + [(1, total - nc - 1)])
    return pp[..., :r * n_sel].reshape(*p.shape[:-1], n_sel, r).sum(-1) + pp[..., r::r]


def _hybrid_mixer(x, w_in, pe_k, w1_k, w2_k, pe_v, w1_v, w2_v, w_out):
    B, S, _ = x.shape
    dt = x.dtype
    f32 = jnp.float32
    s_pad = -(-S // MOBA_BLOCK) * MOBA_BLOCK
    xp = jnp.pad(x, ((0, 0), (0, s_pad - S), (0, 0)))
    n_cmp = s_pad // NSA_CMP_STRIDE - 1
    n_sel = s_pad // NSA_SEL_BLOCK
    n_mb = s_pad // MOBA_BLOCK
    top_n = min(NSA_TOPN, n_sel)
    top_m = min(MOBA_TOPK, n_mb)
    G, HG, dk = NSA_KV_GROUPS, NSA_HPG, HEAD_DIM
    scale = HEAD_DIM ** -0.5
    slope_n, slope_m = _alibi_slopes()

    proj = jnp.einsum('bsd,de->bse', xp, w_in)
    nq, kc, vc, ks, vs, kw, vw, gl, mq, mk, mv = jnp.split(proj, list(np.cumsum(IN_SPLITS)[:-1]), axis=-1)

    nq = nq.reshape(B, s_pad, G, HG, dk).transpose(0, 2, 3, 1, 4) * scale
    kc = _compress(kc.reshape(B, s_pad, G, dk), pe_k, w1_k, w2_k)
    vc = _compress(vc.reshape(B, s_pad, G, dk), pe_v, w1_v, w2_v)
    cmp_end = jnp.arange(n_cmp) * NSA_CMP_STRIDE + (NSA_CMP_LEN - 1)
    heads_g = lambda a: a.reshape(B, s_pad, G, dk).transpose(0, 2, 1, 3)
    ks_blk = heads_g(ks).reshape(B, G, n_sel, NSA_SEL_BLOCK, dk)
    vs_blk = heads_g(vs).reshape(B, G, n_sel, NSA_SEL_BLOCK, dk)
    wpad = ((0, 0), (0, 0), (NSA_WINDOW, 0), (0, 0))
    kw = jnp.pad(heads_g(kw), wpad)
    vw = jnp.pad(heads_g(vw), wpad)
    gates = jax.nn.sigmoid(gl.astype(f32)).reshape(B, s_pad, NSA_HEADS, 3)

    heads_m = lambda a: a.reshape(B, s_pad, MOBA_HEADS, HEAD_DIM).transpose(0, 2, 1, 3)
    mq = heads_m(mq) * scale
    mk = heads_m(mk)
    mv = heads_m(mv)
    mk_blk = mk.reshape(B, MOBA_HEADS, n_mb, MOBA_BLOCK, HEAD_DIM)
    mv_blk = mv.reshape(B, MOBA_HEADS, n_mb, MOBA_BLOCK, HEAD_DIM)
    k_mean = mk_blk.mean(axis=3)

    def chunk(c):
        q0 = c * Q_BLK
        t = q0 + jnp.arange(Q_BLK)
        q = lax.dynamic_slice_in_dim(nq, q0, Q_BLK, axis=3)
        dist_c = (t[:, None] - cmp_end[None, :]).astype(f32)
        s_c = jnp.einsum('bghqd,bgnd->bghqn', q, kc).astype(f32) - slope_n[:, :, None, None] * dist_c
        p_c = _masked_softmax(s_c, cmp_end[None, :] <= t[:, None])
        o_c = jnp.einsum('bghqn,bgnd->bghqd', p_c.astype(dt), vc)
        imp = _cmp_to_sel(p_c.sum(axis=2), n_sel)
        j = jnp.arange(n_sel)[None, :]
        cur = (t // NSA_SEL_BLOCK)[:, None]
        forced = (j == 0) | (j == cur) | (j == cur - 1)
        sel_score = jnp.where(j * NSA_SEL_BLOCK <= t[:, None], imp + jnp.where(forced, NSA_FORCE_BONUS, 0.0), -jnp.inf)
        _, sel_idx = lax.top_k(sel_score, top_n)
        k_g = _gather_blocks(ks_blk, sel_idx)
        v_g = _gather_blocks(vs_blk, sel_idx)
        pos_s = sel_idx[..., None] * NSA_SEL_BLOCK + jnp.arange(NSA_SEL_BLOCK)
        dist_s = (t[None, None, :, None, None] - pos_s)[:, :, None]
        s_s = jnp.einsum('bghqd,bgqnkd->bghqnk', q, k_g).astype(f32) - slope_n[:, :, None, None, None] * dist_s.astype(f32)
        n_keys = top_n * NSA_SEL_BLOCK
        p_s = _masked_softmax(s_s.reshape(B, G, HG, Q_BLK, n_keys), (dist_s >= 0).reshape(B, G, 1, Q_BLK, n_keys))
        o_s = jnp.einsum('bghqk,bgqkd->bghqd', p_s.astype(dt), v_g.reshape(B, G, Q_BLK, n_keys, dk))
        k_w = lax.dynamic_slice_in_dim(kw, q0, NSA_WINDOW + Q_BLK, axis=2)
        v_w = lax.dynamic_slice_in_dim(vw, q0, NSA_WINDOW + Q_BLK, axis=2)
        pos_w = q0 - NSA_WINDOW + jnp.arange(NSA_WINDOW + Q_BLK)
        dist_w = t[:, None] - pos_w[None, :]
        m_w = (dist_w >= 0) & (dist_w < NSA_WINDOW) & (pos_w[None, :] >= 0)
        s_w = jnp.einsum('bghqd,bgkd->bghqk', q, k_w).astype(f32) - slope_n[:, :, None, None] * dist_w.astype(f32)
        p_w = _masked_softmax(s_w, m_w)
        o_w = jnp.einsum('bghqk,bgkd->bghqd', p_w.astype(dt), v_w)
        g = lax.dynamic_slice_in_dim(gates, q0, Q_BLK, axis=1).reshape(B, Q_BLK, G, HG, 3).transpose(0, 2, 3, 1, 4)
        o_nsa = (g[..., 0:1] * o_c + g[..., 1:2] * o_s + g[..., 2:3] * o_w).astype(dt)
        o_nsa = o_nsa.transpose(0, 3, 1, 2, 4).reshape(B, Q_BLK, NSA_Q_COLS)
        qm = lax.dynamic_slice_in_dim(mq, q0, Q_BLK, axis=2)
        cb = q0 // MOBA_BLOCK
        past = jnp.arange(n_mb) < cb
        gate = jnp.where(past, jnp.einsum('bhqd,bhnd->bhqn', qm, k_mean).astype(f32), -jnp.inf)
        _, bidx = lax.top_k(gate, top_m)
        valid = bidx < cb
        km_g = _gather_blocks(mk_blk, bidx)
        vm_g = _gather_blocks(mv_blk, bidx)
        pos_p = bidx[..., None] * MOBA_BLOCK + jnp.arange(MOBA_BLOCK)
        dist_p = (t[:, None, None] - pos_p).astype(f32)
        s_p = jnp.einsum('bhqd,bhqnkd->bhqnk', qm, km_g).astype(f32) - slope_m[:, None, None, None] * dist_p
        n_past = top_m * MOBA_BLOCK
        m_p = jnp.broadcast_to(valid[..., None], s_p.shape).reshape(B, MOBA_HEADS, Q_BLK, n_past)
        k_cur = lax.dynamic_slice_in_dim(mk, cb * MOBA_BLOCK, MOBA_BLOCK, axis=2)
        v_cur = lax.dynamic_slice_in_dim(mv, cb * MOBA_BLOCK, MOBA_BLOCK, axis=2)
        pos_c = cb * MOBA_BLOCK + jnp.arange(MOBA_BLOCK)
        dist_cur = t[:, None] - pos_c[None, :]
        s_cur = jnp.einsum('bhqd,bhkd->bhqk', qm, k_cur).astype(f32) - slope_m[:, None, None] * dist_cur.astype(f32)
        m_cur = jnp.broadcast_to(dist_cur >= 0, s_cur.shape)
        p_m = _masked_softmax(jnp.concatenate([s_p.reshape(B, MOBA_HEADS, Q_BLK, n_past), s_cur], -1),
                              jnp.concatenate([m_p, m_cur], -1)).astype(dt)
        o_m = (jnp.einsum('bhqk,bhqkd->bhqd', p_m[..., :n_past], vm_g.reshape(B, MOBA_HEADS, Q_BLK, n_past, HEAD_DIM))
               + jnp.einsum('bhqk,bhkd->bhqd', p_m[..., n_past:], v_cur))
        o_m = o_m.transpose(0, 2, 1, 3).reshape(B, Q_BLK, MOBA_COLS)
        return jnp.concatenate([o_nsa, o_m], axis=-1)

    out = lax.map(chunk, jnp.arange(s_pad // Q_BLK))
    out = out.transpose(1, 0, 2, 3).reshape(B, s_pad, NSA_Q_COLS + MOBA_COLS)[:, :S]
    return jnp.einsum('bse,ed->bsd', out, w_out)


def _memory_xattn(h, mem, w_q, w_kv, w_o):
    B, S, _ = h.shape
    M = mem.shape[1]
    q = jnp.einsum('bsd,de->bse', h, w_q).reshape(B, S, XA_HEADS, XA_DIM) * XA_DIM ** -0.5
    kv = jnp.einsum('bmd,de->bme', mem, w_kv).reshape(B, M, 2, XA_HEADS, XA_DIM)
    s = jnp.einsum('bshd,bmhd->bhsm', q, kv[:, :, 0]).astype(jnp.float32)
    p = jax.nn.softmax(s, axis=-1).astype(h.dtype)
    o = jnp.einsum('bhsm,bmhd->bshd', p, kv[:, :, 1]).reshape(B, S, XA_HEADS * XA_DIM)
    return jnp.einsum('bse,ed->bsd', o, w_o)


def _peer(h, w_q, sub_keys, exp_u, exp_v):
    B, S, D = h.shape
    dt = h.dtype
    T = B * S
    K = PEER_TOPK

    def block(xb):
        tb = xb.shape[0]
        q = jnp.einsum('td,de->te', xb, w_q).reshape(tb, PEER_HEADS, 2, PEER_QDIM // 2)
        s = jnp.einsum('thcd,hckd->thck', q, sub_keys).astype(jnp.float32)
        sv, si = lax.top_k(s, K)
        cand = (sv[:, :, 0, :, None] + sv[:, :, 1, None, :]).reshape(tb, PEER_HEADS, K * K)
        cidx = (si[:, :, 0, :, None] * PEER_NKEYS + si[:, :, 1, None, :]).reshape(tb, PEER_HEADS, K * K)
        top_v, top_j = lax.top_k(cand, K)
        eidx = jnp.take_along_axis(cidx, top_j, axis=-1)
        gw = jax.nn.softmax(top_v, axis=-1)
        u = exp_u[eidx]
        v = exp_v[eidx]
        a = jax.nn.gelu(jnp.einsum('td,thkd->thk', xb, u).astype(jnp.float32))
        return jnp.einsum('thk,thkd->td', (gw * a).astype(dt), v)

    out = lax.map(block, h.reshape(T // PEER_TOK_BLK, PEER_TOK_BLK, D))
    return out.reshape(B, S, D)


def setup_inputs(seed: int = 0) -> dict:
    key = jax.random.key(seed)
    ks = jax.random.split(key, 24)
    nrm = lambda k, shape, s: jax.random.normal(k, shape, jnp.float32) * s
    L = DEPTH
    return {
        'x': nrm(ks[0], (BATCH, SEQ, D_MODEL), 1.0),
        'mem': nrm(ks[1], (BATCH, MEM_LEN, D_MODEL), 1.0),
        'w_in': nrm(ks[2], (L, D_MODEL, IN_COLS), D_MODEL ** -0.5),
        'cmp_pe_k': nrm(ks[3], (L, NSA_CMP_LEN, HEAD_DIM), 0.1),
        'cmp_w1_k': nrm(ks[4], (L, NSA_CMP_LEN, HEAD_DIM, HEAD_DIM), (NSA_CMP_LEN * HEAD_DIM) ** -0.5),
        'cmp_w2_k': nrm(ks[5], (L, HEAD_DIM, HEAD_DIM), HEAD_DIM ** -0.5),
        'cmp_pe_v': nrm(ks[6], (L, NSA_CMP_LEN, HEAD_DIM), 0.1),
        'cmp_w1_v': nrm(ks[7], (L, NSA_CMP_LEN, HEAD_DIM, HEAD_DIM), (NSA_CMP_LEN * HEAD_DIM) ** -0.5),
        'cmp_w2_v': nrm(ks[8], (L, HEAD_DIM, HEAD_DIM), HEAD_DIM ** -0.5),
        'w_out': nrm(ks[9], (L, NSA_Q_COLS + MOBA_COLS, D_MODEL), DN_BETA * (NSA_Q_COLS + MOBA_COLS) ** -0.5),
        'ln1_g': 1.0 + nrm(ks[10], (L, D_MODEL), 0.01),
        'ln1_b': nrm(ks[11], (L, D_MODEL), 0.01),
        'xa_wq': nrm(ks[12], (L, D_MODEL, XA_HEADS * XA_DIM), D_MODEL ** -0.5),
        'xa_wkv': nrm(ks[13], (L, D_MODEL, 2 * XA_HEADS * XA_DIM), D_MODEL ** -0.5),
        'xa_wo': nrm(ks[14], (L, XA_HEADS * XA_DIM, D_MODEL), DN_BETA * (XA_HEADS * XA_DIM) ** -0.5),
        'ln2_g': 1.0 + nrm(ks[15], (L, D_MODEL), 0.01),
        'ln2_b': nrm(ks[16], (L, D_MODEL), 0.01),
        'peer_wq': nrm(ks[17], (L, D_MODEL, PEER_HEADS * PEER_QDIM), D_MODEL ** -0.5),
        'peer_subkeys': nrm(ks[18], (L, PEER_HEADS, 2, PEER_NKEYS, PEER_QDIM // 2), (PEER_QDIM // 2) ** -0.5),
        'peer_u': nrm(ks[19], (L, PEER_NEXPERTS, D_MODEL), D_MODEL ** -0.5),
        'peer_v': nrm(ks[20], (L, PEER_NEXPERTS, D_MODEL), DN_BETA),
        'ln3_g': 1.0 + nrm(ks[21], (L, D_MODEL), 0.01),
        'ln3_b': nrm(ks[22], (L, D_MODEL), 0.01),
    }


def reference(x, mem, w_in, cmp_pe_k, cmp_w1_k, cmp_w2_k, cmp_pe_v, cmp_w1_v, cmp_w2_v, w_out,
              ln1_g, ln1_b, xa_wq, xa_wkv, xa_wo, ln2_g, ln2_b,
              peer_wq, peer_subkeys, peer_u, peer_v, ln3_g, ln3_b):
    h = x
    for l in range(DEPTH):
        mix = _hybrid_mixer(h, w_in[l], cmp_pe_k[l], cmp_w1_k[l], cmp_w2_k[l],
                            cmp_pe_v[l], cmp_w1_v[l], cmp_w2_v[l], w_out[l])
        h = _layernorm(DN_ALPHA * h + mix, ln1_g[l], ln1_b[l])
        h = _layernorm(DN_ALPHA * h + _memory_xattn(h, mem, xa_wq[l], xa_wkv[l], xa_wo[l]), ln2_g[l], ln2_b[l])
        h = _layernorm(DN_ALPHA * h + _peer(h, peer_wq[l], peer_subkeys[l], peer_u[l], peer_v[l]), ln3_g[l], ln3_b[l])
    return h
```

```python
import functools

import numpy as np
import jax
import jax.numpy as jnp
from jax import lax
from jax.experimental import pallas as pl
from jax.experimental.pallas import tpu as pltpu

F32 = jnp.float32
MXU_DTYPE = jnp.bfloat16
NEG_INF = float("-inf")

LANES = 128
SUBLANES = 8
VMEM_LIMIT = 48 * 1024 * 1024

HEAD_DIM = 128
N_MIX_HEADS = 16
NSA_HEADS = 8
NSA_KV_GROUPS = 2
NSA_HPG = NSA_HEADS // NSA_KV_GROUPS
NSA_CMP_STRIDE = 16
NSA_CMP_LEN = 32
NSA_SEL_BLOCK = 64
NSA_TOPN = 16
NSA_WINDOW = 512
NSA_FORCE_BONUS = 1.0e4
MOBA_HEADS = 8
MOBA_BLOCK = 256
MOBA_TOPK = 3
Q_BLK = 64
XA_HEADS = 4
XA_DIM = 128
PEER_HEADS = 8
PEER_NKEYS = 128
PEER_QDIM = 256
PEER_TOPK = 16
DEPTH = 1
DN_ALPHA = (2 * DEPTH) ** 0.25
LN_EPS = 1e-5

WIN_SPAN = NSA_WINDOW + 2 * Q_BLK
PEER_PICKS = PEER_HEADS * PEER_TOPK
PEER_SLAB = 32
PEER_SLAB_PITCH = 40
PEER_SLOTS = 4
PEER_TOK_TILE = 64


def _cparams(*sem):
    return pltpu.CompilerParams(dimension_semantics=sem, vmem_limit_bytes=VMEM_LIMIT)


def _dot(a, b):
    return jnp.dot(a, b, preferred_element_type=F32)


def _dot_nt(a, b):
    return lax.dot_general(a, b, (((1,), (1,)), ((), ())), preferred_element_type=F32)


def _split_dot(x, m01):
    hi = x.astype(MXU_DTYPE)
    r1 = x - hi.astype(F32)
    mid = r1.astype(MXU_DTYPE)
    lo = (r1 - mid.astype(F32)).astype(MXU_DTYPE)
    return _dot(hi, m01) + _dot(mid, m01) + _dot(lo, m01)


def _masked_softmax(s, mask):
    s = jnp.where(mask, s, NEG_INF)
    m = jnp.max(s, -1, keepdims=True)
    m = jnp.where(jnp.isfinite(m), m, 0.0)
    e = jnp.where(mask, jnp.exp(s - m), 0.0)
    d = jnp.sum(e, -1, keepdims=True)
    return e / jnp.where(d > 0, d, 1.0)


def _online_step(carry, s, v):
    m_i, l_i, acc = carry
    m_new = jnp.maximum(m_i, jnp.max(s, -1, keepdims=True))
    m_safe = jnp.where(m_new == NEG_INF, 0.0, m_new)
    alpha = jnp.exp(m_i - m_safe)
    p = jnp.exp(s - m_safe)
    l_new = alpha * l_i + jnp.sum(p, -1, keepdims=True)
    acc_new = alpha * acc + _dot(p.astype(MXU_DTYPE), v)
    return m_new, l_new, acc_new


def _layernorm(y, g, b):
    mu = jnp.mean(y, -1, keepdims=True)
    var = jnp.mean(jnp.square(y - mu), -1, keepdims=True)
    return (y - mu) * lax.rsqrt(var + LN_EPS) * g + b


def _top_select(score, lane, n_pick):
    def body(_, carry):
        s, sel = carry
        m = jnp.max(s, -1, keepdims=True)
        idx = jnp.min(jnp.where(s == m, lane, float(LANES)), -1, keepdims=True)
        hit = lane == idx
        return jnp.where(hit, NEG_INF, s), jnp.where(hit, 1.0, sel)

    _, sel = lax.fori_loop(0, n_pick, body, (score, jnp.zeros_like(score)))
    return sel


def _mm_kernel(a_ref, b_ref, s_ref, o_ref):
    acc = _dot(a_ref[...], b_ref[...])
    o_ref[...] = (acc * s_ref[...]).astype(o_ref.dtype)


def _matmul(a, b, col_scale, out_dtype, tm, tn, name):
    m, k = a.shape
    n = b.shape[1]
    return pl.pallas_call(
        _mm_kernel,
        out_shape=jax.ShapeDtypeStruct((m, n), out_dtype),
        grid=(m // tm, n // tn),
        in_specs=[pl.BlockSpec((tm, k), lambda i, j: (i, 0)),
                  pl.BlockSpec((k, tn), lambda i, j: (0, j)),
                  pl.BlockSpec((1, tn), lambda i, j: (0, j))],
        out_specs=pl.BlockSpec((tm, tn), lambda i, j: (i, j)),
        compiler_params=_cparams("parallel", "parallel"),
        name=name,
    )(a, b, col_scale)


def _compress_kernel(r_ref, pelo_ref, pehi_ref, w1lo_ref, w1hi_ref, w2_ref, o_ref):
    r = r_ref[0]
    ns = r.shape[0]
    lo = _dot((r + pelo_ref[...]).astype(MXU_DTYPE), w1lo_ref[...])
    hi = _dot((r + pehi_ref[...]).astype(MXU_DTYPE), w1hi_ref[...])
    hid = jax.nn.gelu(lo + pltpu.roll(hi, ns - 1, 0))
    o_ref[0] = _dot(hid.astype(MXU_DTYPE), w2_ref[...]).astype(o_ref.dtype)


def _compress(strips, pe, w1, w2):
    bg, ns, width = strips.shape
    half = NSA_CMP_LEN // 2
    pelo = pe[:half].reshape(1, width)
    pehi = pe[half:].reshape(1, width)
    w1lo = w1[:half].reshape(width, HEAD_DIM).astype(MXU_DTYPE)
    w1hi = w1[half:].reshape(width, HEAD_DIM).astype(MXU_DTYPE)
    const = lambda shape: pl.BlockSpec(shape, lambda i: (0,) * len(shape))
    return pl.pallas_call(
        _compress_kernel,
        out_shape=jax.ShapeDtypeStruct((bg, ns, HEAD_DIM), MXU_DTYPE),
        grid=(bg,),
        in_specs=[pl.BlockSpec((1, ns, width), lambda i: (i, 0, 0)),
                  const((1, width)), const((1, width)),
                  const((width, HEAD_DIM)), const((width, HEAD_DIM)), const((HEAD_DIM, HEAD_DIM))],
        out_specs=pl.BlockSpec((1, ns, HEAD_DIM), lambda i: (i, 0, 0)),
        compiler_params=_cparams("parallel"),
        name="nsa_compress",
    )(strips, pelo, pehi, w1lo, w1hi, w2.astype(MXU_DTYPE))


def _nsa_kernel(q_ref, kc_ref, vc_ref, ks_ref, vs_ref, kw_ref, vw_ref, gl_ref, slope_ref, o_ref, *, top_n):
    c = pl.program_id(2)
    q0 = c * Q_BLK
    rows = NSA_HPG * Q_BLK
    q = q_ref[...]
    qh = jnp.concatenate([q[:, h * HEAD_DIM:(h + 1) * HEAD_DIM] for h in range(NSA_HPG)], axis=0)
    slope = slope_ref[0][:, 0:1]

    def tpos(width):
        return q0 + (lax.broadcasted_iota(jnp.int32, (rows, width), 0) & (Q_BLK - 1))

    kc = kc_ref[0]
    ns = kc.shape[0]
    cend = lax.broadcasted_iota(jnp.int32, (rows, ns), 1) * NSA_CMP_STRIDE + (NSA_CMP_LEN - 1)
    t_c = tpos(ns)
    s_c = _dot_nt(qh, kc) - slope * (t_c - cend).astype(F32)
    p_c = _masked_softmax(s_c, cend <= t_c)
    o_c = _dot(p_c.astype(MXU_DTYPE), vc_ref[0])

    p_sum = p_c[0:Q_BLK]
    for h in range(1, NSA_HPG):
        p_sum = p_sum + p_c[h * Q_BLK:(h + 1) * Q_BLK]
    ratio = NSA_SEL_BLOCK // NSA_CMP_STRIDE
    ci = lax.broadcasted_iota(jnp.int32, (ns, LANES), 0)
    cj = lax.broadcasted_iota(jnp.int32, (ns, LANES), 1) * ratio
    gather01 = jnp.where((ci >= cj - 1) & (ci <= cj + ratio - 1), 1.0, 0.0).astype(MXU_DTYPE)
    imp = _split_dot(p_sum, gather01)
    jlane_i = lax.broadcasted_iota(jnp.int32, (Q_BLK, LANES), 1)
    jlane = jlane_i.astype(F32)
    forced = (jlane_i == 0) | (jlane_i == c) | (jlane_i == c - 1)
    valid = jlane_i <= c
    score = jnp.where(valid, imp + jnp.where(forced, NSA_FORCE_BONUS, 0.0), NEG_INF)
    sel = jnp.where(valid, _top_select(score, jlane, top_n), 0.0)

    klane = lax.broadcasted_iota(jnp.int32, (rows, LANES), 1)
    t_k = tpos(LANES)
    first_half = lax.broadcasted_iota(jnp.int32, (Q_BLK, LANES), 1) < NSA_SEL_BLOCK

    def sel_step(kb, carry):
        k0 = pl.multiple_of(kb * LANES, LANES)
        k = ks_ref[pl.ds(k0, LANES), :]
        v = vs_ref[pl.ds(k0, LANES), :]
        col_a = jnp.max(jnp.where(jlane_i == 2 * kb, sel, 0.0), -1, keepdims=True)
        col_b = jnp.max(jnp.where(jlane_i == 2 * kb + 1, sel, 0.0), -1, keepdims=True)
        picked = jnp.where(first_half, col_a, col_b)
        picked = jnp.concatenate([picked] * NSA_HPG, axis=0) > 0.5
        dist = t_k - (k0 + klane)
        s = _dot_nt(qh, k) - slope * dist.astype(F32)
        s = jnp.where(picked & (dist >= 0), s, NEG_INF)
        return _online_step(carry, s, v)

    init = (jnp.full((rows, 1), NEG_INF, F32), jnp.zeros((rows, 1), F32), jnp.zeros((rows, HEAD_DIM), F32))
    _, l_s, acc_s = lax.fori_loop(0, (c + 2) // 2, sel_step, init)
    o_s = acc_s / jnp.where(l_s > 0, l_s, 1.0)

    w0 = pl.multiple_of(jnp.maximum(q0 - (WIN_SPAN - Q_BLK), 0), Q_BLK)
    t_w = tpos(WIN_SPAN)
    dist_w = t_w - (w0 + lax.broadcasted_iota(jnp.int32, (rows, WIN_SPAN), 1))
    s_w = _dot_nt(qh, kw_ref[pl.ds(w0, WIN_SPAN), :]) - slope * dist_w.astype(F32)
    p_w = _masked_softmax(s_w, (dist_w >= 0) & (dist_w < NSA_WINDOW))
    o_w = _dot(p_w.astype(MXU_DTYPE), vw_ref[pl.ds(w0, WIN_SPAN), :])

    gates = 1.0 / (1.0 + jnp.exp(-gl_ref[...]))
    for h in range(NSA_HPG):
        r = slice(h * Q_BLK, (h + 1) * Q_BLK)
        o = (gates[:, 3 * h:3 * h + 1] * o_c[r] + gates[:, 3 * h + 1:3 * h + 2] * o_s[r]
             + gates[:, 3 * h + 2:3 * h + 3] * o_w[r])
        o_ref[:, h * HEAD_DIM:(h + 1) * HEAD_DIM] = o.astype(o_ref.dtype)


def _nsa(proj, aux, kc, vc, slopes, batch, seq, cols):
    n_chunks = seq // Q_BLK
    ns = kc.shape[1]
    gw = NSA_HPG * HEAD_DIM
    rows = NSA_HPG * Q_BLK
    top_n = min(NSA_TOPN, seq // NSA_SEL_BLOCK)
    kv = lambda base: pl.BlockSpec((seq, HEAD_DIM), lambda b, g, c: (b, base + g))
    cmp_spec = pl.BlockSpec((1, ns, HEAD_DIM), lambda b, g, c: (b * NSA_KV_GROUPS + g, 0, 0))
    return pl.pallas_call(
        functools.partial(_nsa_kernel, top_n=top_n),
        out_shape=jax.ShapeDtypeStruct((batch * seq, NSA_HEADS * HEAD_DIM), MXU_DTYPE),
        grid=(batch, NSA_KV_GROUPS, n_chunks),
        in_specs=[pl.BlockSpec((Q_BLK, gw), lambda b, g, c: (b * n_chunks + c, g)),
                  cmp_spec, cmp_spec,
                  kv(cols["ks"]), kv(cols["vs"]), kv(cols["kw"]), kv(cols["vw"]),
                  pl.BlockSpec((Q_BLK, LANES), lambda b, g, c: (b * n_chunks + c, cols["gl_aux"] + g)),
                  pl.BlockSpec((1, rows, LANES), lambda b, g, c: (g, 0, 0))],
        out_specs=pl.BlockSpec((Q_BLK, gw), lambda b, g, c: (b * n_chunks + c, g)),
        compiler_params=_cparams("parallel", "parallel", "arbitrary"),
        name="nsa_attention",
    )(proj, kc, vc, proj, proj, proj, proj, aux, slopes)


def _moba_kernel(q_ref, k_ref, v_ref, slope_ref, o_ref, kmean_ref, *, top_m):
    cb = pl.program_id(2)
    seq = k_ref.shape[0]
    n_blocks = seq // MOBA_BLOCK

    @pl.when(cb == 0)
    def _():
        kmean_ref[...] = jnp.zeros_like(kmean_ref)
        kf = k_ref[...].astype(F32).reshape(n_blocks, MOBA_BLOCK, HEAD_DIM)
        kmean_ref[0:n_blocks, :] = jnp.sum(kf, axis=1) * (1.0 / MOBA_BLOCK)

    q = q_ref[...]
    slope = slope_ref[0][:, 0:1]
    lane_i = lax.broadcasted_iota(jnp.int32, (MOBA_BLOCK, LANES), 1)
    gate = jnp.where(lane_i < cb, _dot_nt(q, kmean_ref[...].astype(MXU_DTYPE)), NEG_INF)
    sel = jnp.where(lane_i < cb, _top_select(gate, lane_i.astype(F32), top_m), 0.0)

    t = cb * MOBA_BLOCK + lax.broadcasted_iota(jnp.int32, (MOBA_BLOCK, MOBA_BLOCK), 0)
    klane = lax.broadcasted_iota(jnp.int32, (MOBA_BLOCK, MOBA_BLOCK), 1)

    def block_scores(n):
        k0 = pl.multiple_of(n * MOBA_BLOCK, MOBA_BLOCK)
        dist = t - (k0 + klane)
        s = _dot_nt(q, k_ref[pl.ds(k0, MOBA_BLOCK), :]) - slope * dist.astype(F32)
        return s, dist, v_ref[pl.ds(k0, MOBA_BLOCK), :]

    def past_step(n, carry):
        s, _, v = block_scores(n)
        picked = jnp.max(jnp.where(lane_i == n, sel, 0.0), -1, keepdims=True) > 0.5
        return _online_step(carry, jnp.where(picked, s, NEG_INF), v)

    init = (jnp.full((MOBA_BLOCK, 1), NEG_INF, F32), jnp.zeros((MOBA_BLOCK, 1), F32),
            jnp.zeros((MOBA_BLOCK, HEAD_DIM), F32))
    carry = lax.fori_loop(0, cb, past_step, init)
    s, dist, v = block_scores(cb)
    _, l, acc = _online_step(carry, jnp.where(dist >= 0, s, NEG_INF), v)
    o_ref[...] = (acc / jnp.where(l > 0, l, 1.0)).astype(o_ref.dtype)


def _moba(proj, slopes, batch, seq, cols):
    n_blocks = seq // MOBA_BLOCK
    top_m = min(MOBA_TOPK, n_blocks)
    kv = lambda base: pl.BlockSpec((seq, HEAD_DIM), lambda b, h, c: (b, base + h))
    return pl.pallas_call(
        functools.partial(_moba_kernel, top_m=top_m),
        out_shape=jax.ShapeDtypeStruct((batch * seq, MOBA_HEADS * HEAD_DIM), MXU_DTYPE),
        grid=(batch, MOBA_HEADS, n_blocks),
        in_specs=[pl.BlockSpec((MOBA_BLOCK, HEAD_DIM), lambda b, h, c: (b * n_blocks + c, cols["mq"] + h)),
                  kv(cols["mk"]), kv(cols["mv"]),
                  pl.BlockSpec((1, 1, LANES), lambda b, h, c: (h, 0, 0))],
        out_specs=pl.BlockSpec((MOBA_BLOCK, HEAD_DIM), lambda b, h, c: (b * n_blocks + c, h)),
        scratch_shapes=[pltpu.VMEM((LANES, HEAD_DIM), F32)],
        compiler_params=_cparams("parallel", "parallel", "arbitrary"),
        name="moba_attention",
    )(proj, proj, proj, slopes)


def _outproj_ln_kernel(a1_ref, a2_ref, x_ref, w1_ref, w2_ref, g_ref, b_ref, o_ref):
    y = DN_ALPHA * x_ref[...] + _dot(a1_ref[...], w1_ref[...]) + _dot(a2_ref[...], w2_ref[...])
    o_ref[...] = _layernorm(y, g_ref[...], b_ref[...])


def _outproj_ln(a1, a2, x, w1, w2, g, b, tm):
    t, d = x.shape
    k1, k2 = a1.shape[1], a2.shape[1]
    const = lambda shape: pl.BlockSpec(shape, lambda i: (0, 0))
    return pl.pallas_call(
        _outproj_ln_kernel,
        out_shape=jax.ShapeDtypeStruct((t, d), F32),
        grid=(t // tm,),
        in_specs=[pl.BlockSpec((tm, k1), lambda i: (i, 0)), pl.BlockSpec((tm, k2), lambda i: (i, 0)),
                  pl.BlockSpec((tm, d), lambda i: (i, 0)),
                  const((k1, d)), const((k2, d)), const((1, d)), const((1, d))],
        out_specs=pl.BlockSpec((tm, d), lambda i: (i, 0)),
        compiler_params=_cparams("parallel"),
        name="out_proj_ln",
    )(a1, a2, x, w1, w2, g, b)


def _xattn_ln_kernel(h_ref, wq_ref, k_ref, v_ref, wo_ref, g_ref, b_ref, o_ref):
    h = h_ref[...]
    q = (_dot(h.astype(MXU_DTYPE), wq_ref[...]) * (XA_DIM ** -0.5)).astype(MXU_DTYPE)
    outs = []
    for hd in range(XA_HEADS):
        cs = slice(hd * XA_DIM, (hd + 1) * XA_DIM)
        s = _dot_nt(q[:, cs], k_ref[:, cs])
        m = jnp.max(s, -1, keepdims=True)
        e = jnp.exp(s - m)
        p = e / jnp.sum(e, -1, keepdims=True)
        outs.append(_dot(p.astype(MXU_DTYPE), v_ref[:, cs]).astype(MXU_DTYPE))
    o = jnp.concatenate(outs, axis=1)
    y = DN_ALPHA * h + _dot(o, wo_ref[...])
    o_ref[...] = _layernorm(y, g_ref[...], b_ref[...])


def _xattn_ln(h, kv, wq, wo, g, b, batch, seq, mem_len, tm):
    t, d = h.shape
    e = XA_HEADS * XA_DIM
    nt = seq // tm
    const = lambda shape: pl.BlockSpec(shape, lambda bi, i: (0, 0))
    return pl.pallas_call(
        _xattn_ln_kernel,
        out_shape=jax.ShapeDtypeStruct((t, d), F32),
        grid=(batch, nt),
        in_specs=[pl.BlockSpec((tm, d), lambda bi, i: (bi * nt + i, 0)),
                  const((d, e)),
                  pl.BlockSpec((mem_len, e), lambda bi, i: (bi, 0)),
                  pl.BlockSpec((mem_len, e), lambda bi, i: (bi, 1)),
                  const((e, d)), const((1, d)), const((1, d))],
        out_specs=pl.BlockSpec((tm, d), lambda bi, i: (bi * nt + i, 0)),
        compiler_params=_cparams("parallel", "parallel"),
        name="xattn_ln",
    )(h, wq, kv, kv, wo, g, b)


def _peer_route_kernel(h_ref, wq_ref, sk_ref, gw_ref, ei_ref):
    hh = pl.program_id(1)
    tm = h_ref.shape[0]
    k_top = PEER_TOPK
    wide = k_top * k_top

    @pl.when(hh == 0)
    def _():
        gw_ref[...] = jnp.zeros_like(gw_ref)
        ei_ref[...] = jnp.zeros_like(ei_ref)

    q = _dot(h_ref[...].astype(MXU_DTYPE), wq_ref[...]).astype(MXU_DTYPE)
    lane_i = lax.broadcasted_iota(jnp.int32, (tm, LANES), 1)
    lane = lane_i.astype(F32)
    wlane_i = lax.broadcasted_iota(jnp.int32, (tm, wide), 1)
    wlane = wlane_i.astype(F32)

    grids = []
    for half in range(2):
        s0 = _dot_nt(q[:, half * PEER_NKEYS:(half + 1) * PEER_NKEYS], sk_ref[0, half])
        slot_of = (wlane_i >> 4) if half == 0 else (wlane_i & (k_top - 1))

        def pick(k, carry, slot_of=slot_of):
            s, vals, idxs = carry
            m = jnp.max(s, -1, keepdims=True)
            idx = jnp.min(jnp.where(s == m, lane, float(LANES)), -1, keepdims=True)
            here = slot_of == k
            return (jnp.where(lane == idx, NEG_INF, s), jnp.where(here, m, vals), jnp.where(here, idx, idxs))

        _, vals, idxs = lax.fori_loop(0, k_top, pick, (s0, jnp.zeros((tm, wide), F32), jnp.zeros((tm, wide), F32)))
        grids.append((vals, idxs))
    cand = grids[0][0] + grids[1][0]
    cidx = grids[0][1] * float(PEER_NKEYS) + grids[1][1]

    def pick_expert(k, carry):
        cd, tv, te = carry
        m = jnp.max(cd, -1, keepdims=True)
        j = jnp.min(jnp.where(cd == m, wlane, float(wide)), -1, keepdims=True)
        hit = wlane == j
        e = jnp.sum(jnp.where(hit, cidx, 0.0), -1, keepdims=True)
        here = lane_i == hh * k_top + k
        return jnp.where(hit, NEG_INF, cd), jnp.where(here, m, tv), jnp.where(here, e, te)

    _, tv, te = lax.fori_loop(0, k_top, pick_expert,
                              (cand, jnp.zeros((tm, LANES), F32), jnp.zeros((tm, LANES), F32)))
    mine = (lane_i >> 4) == hh
    mx = jnp.max(jnp.where(mine, tv, NEG_INF), -1, keepdims=True)
    ex = jnp.where(mine, jnp.exp(tv - mx), 0.0)
    gw = ex / jnp.sum(ex, -1, keepdims=True)
    gw_ref[...] = jnp.where(mine, gw, gw_ref[...])
    ei_ref[...] = jnp.where(mine, te.astype(jnp.int32), ei_ref[...])


def _peer_route(h, wq, sub_keys, tm):
    t, d = h.shape
    return pl.pallas_call(
        _peer_route_kernel,
        out_shape=(jax.ShapeDtypeStruct((t, PEER_PICKS), F32), jax.ShapeDtypeStruct((t, PEER_PICKS), jnp.int32)),
        grid=(t // tm, PEER_HEADS),
        in_specs=[pl.BlockSpec((tm, d), lambda i, hh: (i, 0)),
                  pl.BlockSpec((d, PEER_QDIM), lambda i, hh: (0, hh)),
                  pl.BlockSpec((1, 2, PEER_NKEYS, PEER_QDIM // 2), lambda i, hh: (hh, 0, 0, 0))],
        out_specs=(pl.BlockSpec((tm, PEER_PICKS), lambda i, hh: (i, 0)),
                   pl.BlockSpec((tm, PEER_PICKS), lambda i, hh: (i, 0))),
        compiler_params=_cparams("parallel", "arbitrary"),
        name="peer_route",
    )(h, wq, sub_keys)


def _peer_expert_kernel(ei_ref, gw_ref, x_ref, uv_ref, o_ref, buf_ref, sem_ref):
    n_tok = x_ref.shape[0]
    half = PEER_SLAB // 2
    groups = PEER_PICKS // SUBLANES

    def slab_copy(tok, j, slot):
        dst = buf_ref.at[slot, pl.ds(pl.multiple_of(j * PEER_SLAB_PITCH, SUBLANES), PEER_SLAB)]
        return pltpu.make_async_copy(uv_ref.at[ei_ref[tok, j]], dst, sem_ref.at[slot])

    def issue(tok, slot):
        def group(jg, _):
            for jj in range(SUBLANES):
                slab_copy(tok, jg * SUBLANES + jj, slot).start()
            return 0
        lax.fori_loop(0, groups, group, 0)

    def wait(tok, slot):
        def group(jg, _):
            for jj in range(SUBLANES):
                slab_copy(tok, jg * SUBLANES + jj, slot).wait()
            return 0
        lax.fori_loop(0, groups, group, 0)

    for tok in range(PEER_SLOTS - 1):
        issue(tok, tok)

    gw_t = gw_ref[0]
    tok_lane = lax.broadcasted_iota(jnp.int32, gw_t.shape, 1)

    def per_token(tok, _):
        slot = tok % PEER_SLOTS
        nxt = tok + PEER_SLOTS - 1

        @pl.when(nxt < n_tok)
        def _():
            issue(nxt, nxt % PEER_SLOTS)

        wait(tok, slot)
        slabs = buf_ref.at[slot]

        def rows(j0, r):
            return slabs[pl.ds(j0 * PEER_SLAB_PITCH + r, SUBLANES, stride=PEER_SLAB_PITCH), :]

        xs = [x_ref[tok, pl.ds(r, 1), :] for r in range(half)]
        parts = []
        for jg in range(groups):
            acc = rows(jg * SUBLANES, 0) * xs[0]
            for r in range(1, half):
                acc = acc + rows(jg * SUBLANES, r) * xs[r]
            parts.append(jnp.sum(acc, -1, keepdims=True))
        a = jnp.concatenate(parts, axis=0)
        gw_col = jnp.sum(jnp.where(tok_lane == tok, gw_t, 0.0), -1, keepdims=True)
        w = gw_col * jax.nn.gelu(a)
        for r in range(half):
            acc = rows(0, half + r) * w[0:SUBLANES]
            for jg in range(1, groups):
                acc = acc + rows(jg * SUBLANES, half + r) * w[jg * SUBLANES:(jg + 1) * SUBLANES]
            o_ref[tok, pl.ds(r, 1), :] = jnp.sum(acc, 0, keepdims=True)
        return 0

    lax.fori_loop(0, n_tok, per_token, 0)


def _peer_experts(eidx, gw, x3, uv):
    t = x3.shape[0]
    tt = PEER_TOK_TILE
    half = PEER_SLAB // 2
    return pl.pallas_call(
        _peer_expert_kernel,
        out_shape=jax.ShapeDtypeStruct((t, half, LANES), F32),
        grid=(t // tt,),
        in_specs=[pl.BlockSpec((tt, PEER_PICKS), lambda i: (i, 0), memory_space=pltpu.SMEM),
                  pl.BlockSpec((1, PEER_PICKS, tt), lambda i: (i, 0, 0)),
                  pl.BlockSpec((tt, half, LANES), lambda i: (i, 0, 0)),
                  pl.BlockSpec(memory_space=pl.ANY)],
        out_specs=pl.BlockSpec((tt, half, LANES), lambda i: (i, 0, 0)),
        scratch_shapes=[pltpu.VMEM((PEER_SLOTS, PEER_PICKS * PEER_SLAB_PITCH, LANES), F32),
                        pltpu.SemaphoreType.DMA((PEER_SLOTS,))],
        compiler_params=_cparams("arbitrary"),
        name="peer_experts",
    )(eidx, gw.reshape(t // tt, tt, PEER_PICKS).transpose(0, 2, 1), x3, uv)


def _add_ln_kernel(h_ref, f_ref, g_ref, b_ref, o_ref):
    o_ref[...] = _layernorm(DN_ALPHA * h_ref[...] + f_ref[...], g_ref[...], b_ref[...])


def _add_ln(h, f, g, b, tm):
    t, d = h.shape
    row = pl.BlockSpec((tm, d), lambda i: (i, 0))
    const = pl.BlockSpec((1, d), lambda i: (0, 0))
    return pl.pallas_call(
        _add_ln_kernel,
        out_shape=jax.ShapeDtypeStruct((t, d), F32),
        grid=(t // tm,),
        in_specs=[row, row, const, const],
        out_specs=row,
        compiler_params=_cparams("parallel"),
        name="add_ln",
    )(h, f, g, b)


def _alibi_slopes():
    s = (2.0 ** (-8.0 * (np.arange(N_MIX_HEADS) + 1) / N_MIX_HEADS)).astype(np.float32)
    return s[0::2], s[1::2]


def _mixer(x2, batch, seq, w_in, pe_k, w1_k, w2_k, pe_v, w1_v, w2_v):
    d = x2.shape[1]
    hd = HEAD_DIM
    sizes = [NSA_HEADS * hd] + [NSA_KV_GROUPS * hd] * 6 + [NSA_HEADS * 3] + [MOBA_HEADS * hd] * 3
    offs = np.concatenate([[0], np.cumsum(sizes)])
    sec = {n: w_in[:, offs[i]:offs[i + 1]] for i, n in enumerate(
        ["nq", "kc", "vc", "ks", "vs", "kw", "vw", "gl", "mq", "mk", "mv"])}
    order = ["nq", "ks", "vs", "kw", "vw", "mq", "mk", "mv"]
    cols, at = {}, 0
    for n in order:
        cols[n] = at // LANES
        at += sec[n].shape[1]
    w_main = jnp.concatenate([sec[n] for n in order], axis=1).astype(MXU_DTYPE)
    scale = hd ** -0.5
    col_scale = jnp.concatenate([
        jnp.full((1, w.shape[1]), scale if n in ("nq", "mq") else 1.0, F32)
        for n, w in ((n, sec[n]) for n in order)], axis=1)
    per_group = NSA_HPG * 3
    gl_cols = [jnp.pad(sec["gl"][:, g * per_group:(g + 1) * per_group], ((0, 0), (0, LANES - per_group)))
               for g in range(NSA_KV_GROUPS)]
    w_aux = jnp.concatenate([sec["kc"], sec["vc"]] + gl_cols, axis=1).astype(MXU_DTYPE)
    cols["gl_aux"] = (2 * NSA_KV_GROUPS * hd) // LANES

    xb = x2.astype(MXU_DTYPE)
    tm = 512 if x2.shape[0] % 512 == 0 else 256
    proj = _matmul(xb, w_main, col_scale, MXU_DTYPE, tm, 512, "in_proj")
    aux = _matmul(xb, w_aux, jnp.ones((1, w_aux.shape[1]), F32), F32, tm, 256, "in_proj_aux")

    ns = seq // NSA_CMP_STRIDE

    def strips(col0):
        raw = aux[:, col0:col0 + NSA_KV_GROUPS * hd].reshape(batch, seq, NSA_KV_GROUPS, hd)
        return raw.transpose(0, 2, 1, 3).reshape(batch * NSA_KV_GROUPS, ns, NSA_CMP_STRIDE * hd)

    kc = _compress(strips(0), pe_k, w1_k, w2_k)
    vc = _compress(strips(NSA_KV_GROUPS * hd), pe_v, w1_v, w2_v)

    slope_n, slope_m = _alibi_slopes()
    sn = np.repeat(slope_n.reshape(NSA_KV_GROUPS, NSA_HPG), Q_BLK, axis=1)
    sn = jnp.asarray(np.broadcast_to(sn[:, :, None], sn.shape + (LANES,)).copy())
    sm = jnp.asarray(np.broadcast_to(slope_m[:, None, None], (MOBA_HEADS, 1, LANES)).copy())

    o_nsa = _nsa(proj, aux, kc, vc, sn, batch, seq, cols)
    o_moba = _moba(proj, sm, batch, seq, cols)
    return o_nsa, o_moba


def _memory_xattn_ln(h, mem2, batch, seq, wq, wkv, wo, g, b):
    mem_len = mem2.shape[0] // batch
    e2 = wkv.shape[1]
    kv = _matmul(mem2.astype(MXU_DTYPE), wkv.astype(MXU_DTYPE), jnp.ones((1, e2), F32), MXU_DTYPE,
                 mem_len, e2 // 2, "xattn_kv")
    return _xattn_ln(h, kv, wq.astype(MXU_DTYPE), wo.astype(MXU_DTYPE), g, b, batch, seq, mem_len, 256)


def _peer_ln(h, wq, sub_keys, exp_u, exp_v, g, b):
    t, d = h.shape
    gw, eidx = _peer_route(h, wq.astype(MXU_DTYPE), sub_keys.astype(MXU_DTYPE), 256)
    half = PEER_SLAB // 2
    n_exp = exp_u.shape[0]
    uv = jnp.concatenate([exp_u.reshape(n_exp, half, LANES), exp_v.reshape(n_exp, half, LANES)], axis=1)
    f = _peer_experts(eidx, gw, h.reshape(t, half, LANES), uv).reshape(t, d)
    return _add_ln(h, f, g, b, 256)


def kernel(x, mem, w_in, cmp_pe_k, cmp_w1_k, cmp_w2_k, cmp_pe_v, cmp_w1_v, cmp_w2_v, w_out, ln1_g, ln1_b,
           xa_wq, xa_wkv, xa_wo, ln2_g, ln2_b, peer_wq, peer_subkeys, peer_u, peer_v, ln3_g, ln3_b):
    batch, seq, d = x.shape
    assert seq % MOBA_BLOCK == 0 and WIN_SPAN <= seq <= NSA_SEL_BLOCK * LANES
    assert d == PEER_SLAB // 2 * LANES and w_in.shape[0] == DEPTH
    row = lambda v: v.reshape(1, d)
    h = x.reshape(batch * seq, d)
    mem2 = mem.reshape(-1, d)
    for l in range(DEPTH):
        o_nsa, o_moba = _mixer(h, batch, seq, w_in[l], cmp_pe_k[l], cmp_w1_k[l], cmp_w2_k[l],
                               cmp_pe_v[l], cmp_w1_v[l], cmp_w2_v[l])
        wo = w_out[l].astype(MXU_DTYPE)
        k1 = o_nsa.shape[1]
        h = _outproj_ln(o_nsa, o_moba, h, wo[:k1], wo[k1:], row(ln1_g[l]), row(ln1_b[l]), 256)
        h = _memory_xattn_ln(h, mem2, batch, seq, xa_wq[l], xa_wkv[l], xa_wo[l], row(ln2_g[l]), row(ln2_b[l]))
        h = _peer_ln(h, peer_wq[l], peer_subkeys[l], peer_u[l], peer_v[l], row(ln3_g[l]), row(ln3_b[l]))
    return h.reshape(batch, seq, d)
```

```python
import functools

import numpy as np
import jax
import jax.numpy as jnp
from jax import lax
from jax.experimental import pallas as pl
from jax.experimental.pallas import tpu as pltpu

F32 = jnp.float32
MXU_DTYPE = jnp.bfloat16
NEG_INF = float("-inf")

LANES = 128
SUBLANES = 8
VMEM_LIMIT = 48 * 1024 * 1024

HEAD_DIM = 128
N_MIX_HEADS = 16
NSA_HEADS = 8
NSA_KV_GROUPS = 2
NSA_HPG = NSA_HEADS // NSA_KV_GROUPS
NSA_CMP_STRIDE = 16
NSA_CMP_LEN = 32
NSA_SEL_BLOCK = 64
NSA_TOPN = 16
NSA_WINDOW = 512
NSA_FORCE_BONUS = 1.0e4
MOBA_HEADS = 8
MOBA_BLOCK = 256
MOBA_TOPK = 3
Q_BLK = 64
XA_HEADS = 4
XA_DIM = 128
PEER_HEADS = 8
PEER_NKEYS = 128
PEER_QDIM = 256
PEER_TOPK = 16
DEPTH = 1
DN_ALPHA = (2 * DEPTH) ** 0.25
LN_EPS = 1e-5

NSA_ROWS = NSA_HPG * Q_BLK
WIN_SPAN = NSA_WINDOW + 2 * Q_BLK
SEL_STEP_BLOCKS = SUBLANES
SEL_STEP = SEL_STEP_BLOCKS * NSA_SEL_BLOCK
MOBA_STEP_BLOCKS = 4
MOBA_STEP = MOBA_STEP_BLOCKS * MOBA_BLOCK
PEER_PICKS = PEER_HEADS * PEER_TOPK
PEER_SLAB = 32
PEER_SLAB_PITCH = 40
PEER_SLOTS = 4
PEER_TOK_TILE = LANES


def _cparams(*sem):
    return pltpu.CompilerParams(dimension_semantics=sem, vmem_limit_bytes=VMEM_LIMIT)


def _dot(a, b):
    return jnp.dot(a, b, preferred_element_type=F32)


def _dot_nt(a, b):
    return lax.dot_general(a, b, (((1,), (1,)), ((), ())), preferred_element_type=F32)


def _split3(x):
    hi = x.astype(MXU_DTYPE)
    r1 = x - hi.astype(F32)
    mid = r1.astype(MXU_DTYPE)
    lo = (r1 - mid.astype(F32)).astype(MXU_DTYPE)
    return hi, mid, lo


def _masked_softmax(s, mask):
    s = jnp.where(mask, s, NEG_INF)
    m = jnp.max(s, -1, keepdims=True)
    m = jnp.where(jnp.isfinite(m), m, 0.0)
    e = jnp.where(mask, jnp.exp(s - m), 0.0)
    d = jnp.sum(e, -1, keepdims=True)
    return e / jnp.where(d > 0, d, 1.0)


def _online_step_t(carry, s, row_shift, vt):
    m_i, l_i, acc = carry
    m_new = jnp.maximum(m_i, jnp.max(s, 0, keepdims=True) - row_shift)
    m_safe = jnp.where(m_new == NEG_INF, 0.0, m_new)
    alpha = jnp.exp(m_i - m_safe)
    p = jnp.exp(s - (m_safe + row_shift))
    l_new = alpha * l_i + jnp.sum(p, 0, keepdims=True)
    acc_new = alpha * acc + _dot(vt, p.astype(MXU_DTYPE))
    return m_new, l_new, acc_new


def _flash_init(n_q):
    return (jnp.full((1, n_q), NEG_INF, F32), jnp.zeros((1, n_q), F32), jnp.zeros((HEAD_DIM, n_q), F32))


def _layernorm(y, g, b):
    mu = jnp.mean(y, -1, keepdims=True)
    var = jnp.mean(jnp.square(y - mu), -1, keepdims=True)
    return (y - mu) * lax.rsqrt(var + LN_EPS) * g + b


def _top_select_t(score, row, n_pick):
    sentinel = float(score.shape[0])

    def body(_, carry):
        s, sel = carry
        m = jnp.max(s, 0, keepdims=True)
        idx = jnp.min(jnp.where(s == m, row, sentinel), 0, keepdims=True)
        hit = row == idx
        return jnp.where(hit, NEG_INF, s), jnp.where(hit, 1.0, sel)

    _, sel = lax.fori_loop(0, n_pick, body, (score, jnp.zeros_like(score)))
    return sel


def _top_values_t(score, row, n_pick):
    sentinel = float(score.shape[0])
    n = score.shape[1]
    out_row = lax.broadcasted_iota(jnp.int32, (n_pick, n), 0)

    def body(k, carry):
        s, vals, idxs = carry
        m = jnp.max(s, 0, keepdims=True)
        idx = jnp.min(jnp.where(s == m, row, sentinel), 0, keepdims=True)
        here = out_row == k
        return jnp.where(row == idx, NEG_INF, s), jnp.where(here, m, vals), jnp.where(here, idx, idxs)

    zeros = jnp.zeros((n_pick, n), F32)
    _, vals, idxs = lax.fori_loop(0, n_pick, body, (score, zeros, zeros))
    return vals, idxs


def _mm_kernel(a_ref, b_ref, s_ref, o_ref):
    acc = _dot(a_ref[...], b_ref[...])
    o_ref[...] = (acc * s_ref[...]).astype(o_ref.dtype)


def _matmul(a, b, col_scale, out_dtype, tm, tn, name):
    m, k = a.shape
    n = b.shape[1]
    return pl.pallas_call(
        _mm_kernel,
        out_shape=jax.ShapeDtypeStruct((m, n), out_dtype),
        grid=(m // tm, n // tn),
        in_specs=[pl.BlockSpec((tm, k), lambda i, j: (i, 0)),
                  pl.BlockSpec((k, tn), lambda i, j: (0, j)),
                  pl.BlockSpec((1, tn), lambda i, j: (0, j))],
        out_specs=pl.BlockSpec((tm, tn), lambda i, j: (i, j)),
        compiler_params=_cparams("parallel", "parallel"),
        name=name,
    )(a, b, col_scale)


def _compress_kernel(r_ref, pelo_ref, pehi_ref, w1lo_ref, w1hi_ref, w2_ref, o_ref):
    r = r_ref[0]
    ns = r.shape[0]
    lo = _dot((r + pelo_ref[...]).astype(MXU_DTYPE), w1lo_ref[...])
    hi = _dot((r + pehi_ref[...]).astype(MXU_DTYPE), w1hi_ref[...])
    hid = jax.nn.gelu(lo + pltpu.roll(hi, ns - 1, 0))
    o_ref[0] = _dot(hid.astype(MXU_DTYPE), w2_ref[...]).astype(o_ref.dtype)


def _compress(strips, pe, w1, w2):
    bg, ns, width = strips.shape
    half = NSA_CMP_LEN // 2
    pelo = pe[:half].reshape(1, width)
    pehi = pe[half:].reshape(1, width)
    w1lo = w1[:half].reshape(width, HEAD_DIM).astype(MXU_DTYPE)
    w1hi = w1[half:].reshape(width, HEAD_DIM).astype(MXU_DTYPE)
    const = lambda shape: pl.BlockSpec(shape, lambda i: (0,) * len(shape))
    return pl.pallas_call(
        _compress_kernel,
        out_shape=jax.ShapeDtypeStruct((bg, ns, HEAD_DIM), MXU_DTYPE),
        grid=(bg,),
        in_specs=[pl.BlockSpec((1, ns, width), lambda i: (i, 0, 0)),
                  const((1, width)), const((1, width)),
                  const((width, HEAD_DIM)), const((width, HEAD_DIM)), const((HEAD_DIM, HEAD_DIM))],
        out_specs=pl.BlockSpec((1, ns, HEAD_DIM), lambda i: (i, 0, 0)),
        compiler_params=_cparams("parallel"),
        name="nsa_compress",
    )(strips, pelo, pehi, w1lo, w1hi, w2.astype(MXU_DTYPE))


def _nsa_kernel(q_ref, kc_ref, vc_ref, ks_ref, vst_ref, kw_ref, vw_ref, gl_ref, slope_ref, slope_t_ref,
                o_ref, selt_ref, *, top_n):
    c = pl.program_id(2)
    q0 = c * Q_BLK
    rows = NSA_ROWS
    q = q_ref[...]
    qh = jnp.concatenate([q[:, h * HEAD_DIM:(h + 1) * HEAD_DIM] for h in range(NSA_HPG)], axis=0)
    slope = slope_ref[0][:, 0:1]
    slope_t = slope_t_ref[0]

    def tpos(width):
        return q0 + (lax.broadcasted_iota(jnp.int32, (rows, width), 0) & (Q_BLK - 1))

    kc = kc_ref[0]
    ns = kc.shape[0]
    cend = lax.broadcasted_iota(jnp.int32, (rows, ns), 1) * NSA_CMP_STRIDE + (NSA_CMP_LEN - 1)
    t_c = tpos(ns)
    s_c = _dot_nt(qh, kc) - slope * (t_c - cend).astype(F32)
    p_c = _masked_softmax(s_c, cend <= t_c)
    o_c = _dot(p_c.astype(MXU_DTYPE), vc_ref[0])

    p_sum = p_c[0:Q_BLK]
    for h in range(1, NSA_HPG):
        p_sum = p_sum + p_c[h * Q_BLK:(h + 1) * Q_BLK]
    p_two = jnp.concatenate([p_sum, p_sum], axis=0)
    ratio = NSA_SEL_BLOCK // NSA_CMP_STRIDE
    gj = lax.broadcasted_iota(jnp.int32, (LANES, ns), 0) * ratio
    gi = lax.broadcasted_iota(jnp.int32, (LANES, ns), 1)
    gather01 = jnp.where((gi >= gj - 1) & (gi <= gj + ratio - 1), 1.0, 0.0).astype(MXU_DTYPE)
    hi, mid, lo = _split3(p_two)
    imp_t = _dot_nt(gather01, hi) + _dot_nt(gather01, mid) + _dot_nt(gather01, lo)
    jrow_i = lax.broadcasted_iota(jnp.int32, (LANES, LANES), 0)
    forced = (jrow_i == 0) | (jrow_i == c) | (jrow_i == c - 1)
    valid = jrow_i <= c
    score_t = jnp.where(valid, imp_t + jnp.where(forced, NSA_FORCE_BONUS, 0.0), NEG_INF)
    sel_t = jnp.where(valid, _top_select_t(score_t, jrow_i.astype(F32), top_n), 0.0)
    selt_ref[...] = jnp.concatenate([sel_t, sel_t], axis=1)

    off = (lax.broadcasted_iota(jnp.int32, (SEL_STEP, rows), 0)
           - (lax.broadcasted_iota(jnp.int32, (SEL_STEP, rows), 1) & (Q_BLK - 1)))
    bias_t = slope_t * (-off).astype(F32)

    def sel_step(kb, carry, causal):
        k0 = pl.multiple_of(kb * SEL_STEP, SEL_STEP)
        s = _dot_nt(ks_ref[pl.ds(k0, SEL_STEP), :], qh) - bias_t
        picks = selt_ref[pl.ds(pl.multiple_of(kb * SEL_STEP_BLOCKS, SEL_STEP_BLOCKS), SEL_STEP_BLOCKS), :]
        pieces = []
        for i in range(SEL_STEP_BLOCKS):
            r = slice(i * NSA_SEL_BLOCK, (i + 1) * NSA_SEL_BLOCK)
            ok = picks[i:i + 1, :] > 0.5
            if causal:
                ok = ok & (off[r] <= q0 - k0)
            pieces.append(jnp.where(ok, s[r], NEG_INF))
        s = jnp.concatenate(pieces, axis=0)
        shift = slope_t * (q0 - k0).astype(F32)
        return _online_step_t(carry, s, shift, vst_ref[:, pl.ds(k0, SEL_STEP)])

    last = c // SEL_STEP_BLOCKS
    carry = lax.fori_loop(0, last, functools.partial(sel_step, causal=False), _flash_init(rows))
    _, l_s, acc_s = sel_step(last, carry, causal=True)
    o_s = (acc_s / jnp.where(l_s > 0, l_s, 1.0)).T

    w0 = pl.multiple_of(jnp.maximum(q0 - (WIN_SPAN - Q_BLK), 0), Q_BLK)
    t_w = tpos(WIN_SPAN)
    dist_w = t_w - (w0 + lax.broadcasted_iota(jnp.int32, (rows, WIN_SPAN), 1))
    s_w = _dot_nt(qh, kw_ref[pl.ds(w0, WIN_SPAN), :]) - slope * dist_w.astype(F32)
    p_w = _masked_softmax(s_w, (dist_w >= 0) & (dist_w < NSA_WINDOW))
    o_w = _dot(p_w.astype(MXU_DTYPE), vw_ref[pl.ds(w0, WIN_SPAN), :])

    gates = 1.0 / (1.0 + jnp.exp(-gl_ref[...]))
    for h in range(NSA_HPG):
        r = slice(h * Q_BLK, (h + 1) * Q_BLK)
        o = (gates[:, 3 * h:3 * h + 1] * o_c[r] + gates[:, 3 * h + 1:3 * h + 2] * o_s[r]
             + gates[:, 3 * h + 2:3 * h + 3] * o_w[r])
        o_ref[:, h * HEAD_DIM:(h + 1) * HEAD_DIM] = o.astype(o_ref.dtype)


def _nsa(proj, aux, kc, vc, vs_t, slopes, slopes_t, batch, seq, cols):
    n_chunks = seq // Q_BLK
    ns = kc.shape[1]
    gw = NSA_HPG * HEAD_DIM
    top_n = min(NSA_TOPN, seq // NSA_SEL_BLOCK)
    kv = lambda base: pl.BlockSpec((seq, HEAD_DIM), lambda b, g, c: (b, base + g))
    cmp_spec = pl.BlockSpec((1, ns, HEAD_DIM), lambda b, g, c: (b * NSA_KV_GROUPS + g, 0, 0))
    return pl.pallas_call(
        functools.partial(_nsa_kernel, top_n=top_n),
        out_shape=jax.ShapeDtypeStruct((batch * seq, NSA_HEADS * HEAD_DIM), MXU_DTYPE),
        grid=(batch, NSA_KV_GROUPS, n_chunks),
        in_specs=[pl.BlockSpec((Q_BLK, gw), lambda b, g, c: (b * n_chunks + c, g)),
                  cmp_spec, cmp_spec,
                  kv(cols["ks"]),
                  pl.BlockSpec((HEAD_DIM, seq), lambda b, g, c: (b * NSA_KV_GROUPS + g, 0)),
                  kv(cols["kw"]), kv(cols["vw"]),
                  pl.BlockSpec((Q_BLK, LANES), lambda b, g, c: (b * n_chunks + c, cols["gl_aux"] + g)),
                  pl.BlockSpec((1, NSA_ROWS, LANES), lambda b, g, c: (g, 0, 0)),
                  pl.BlockSpec((1, 1, NSA_ROWS), lambda b, g, c: (g, 0, 0))],
        out_specs=pl.BlockSpec((Q_BLK, gw), lambda b, g, c: (b * n_chunks + c, g)),
        scratch_shapes=[pltpu.VMEM((LANES, NSA_ROWS), F32)],
        compiler_params=_cparams("parallel", "parallel", "arbitrary"),
        name="nsa_attention",
    )(proj, kc, vc, proj, vs_t, proj, proj, aux, slopes, slopes_t)


def _moba_kernel(q_ref, k_ref, vt_ref, slope_ref, o_ref, kmean_ref, selt_ref, *, top_m):
    cb = pl.program_id(2)
    seq = k_ref.shape[0]
    n_blocks = seq // MOBA_BLOCK
    nq = MOBA_BLOCK

    @pl.when(cb == 0)
    def _():
        kmean_ref[...] = jnp.zeros_like(kmean_ref)
        kf = k_ref[...].astype(F32).reshape(n_blocks, MOBA_BLOCK, HEAD_DIM)
        kmean_ref[0:n_blocks, :] = jnp.sum(kf, axis=1) * (1.0 / MOBA_BLOCK)

    q = q_ref[...]
    slope = slope_ref[0][:, 0:1]
    brow_i = lax.broadcasted_iota(jnp.int32, (LANES, nq), 0)
    gate_t = jnp.where(brow_i < cb, _dot_nt(kmean_ref[...].astype(MXU_DTYPE), q), NEG_INF)
    selt_ref[...] = jnp.where(brow_i < cb, _top_select_t(gate_t, brow_i.astype(F32), top_m), 0.0)

    off = (lax.broadcasted_iota(jnp.int32, (MOBA_STEP, nq), 0)
           - lax.broadcasted_iota(jnp.int32, (MOBA_STEP, nq), 1))
    bias_t = slope * (-off).astype(F32)
    q0 = cb * MOBA_BLOCK

    def past_step(st, carry):
        k0 = pl.multiple_of(st * MOBA_STEP, MOBA_STEP)
        s = _dot_nt(k_ref[pl.ds(k0, MOBA_STEP), :], q) - bias_t
        pair = selt_ref[pl.ds(pl.multiple_of((st // 2) * SUBLANES, SUBLANES), SUBLANES), :]
        picks = jnp.where(st % 2 == 0, pair[0:MOBA_STEP_BLOCKS], pair[MOBA_STEP_BLOCKS:2 * MOBA_STEP_BLOCKS])
        pieces = [jnp.where(picks[i:i + 1, :] > 0.5, s[i * MOBA_BLOCK:(i + 1) * MOBA_BLOCK], NEG_INF)
                  for i in range(MOBA_STEP_BLOCKS)]
        shift = slope * (q0 - k0).astype(F32)
        return _online_step_t(carry, jnp.concatenate(pieces, axis=0), shift, vt_ref[:, pl.ds(k0, MOBA_STEP)])

    n_steps = (cb + MOBA_STEP_BLOCKS - 1) // MOBA_STEP_BLOCKS
    carry = lax.fori_loop(0, n_steps, past_step, _flash_init(nq))
    kc0 = pl.multiple_of(q0, MOBA_BLOCK)
    s = _dot_nt(k_ref[pl.ds(kc0, MOBA_BLOCK), :], q) - bias_t[0:MOBA_BLOCK]
    s = jnp.where(off[0:MOBA_BLOCK] <= 0, s, NEG_INF)
    _, l, acc = _online_step_t(carry, s, jnp.zeros((1, 1), F32), vt_ref[:, pl.ds(kc0, MOBA_BLOCK)])
    o_ref[...] = (acc / jnp.where(l > 0, l, 1.0)).T.astype(o_ref.dtype)


def _moba(proj, mv_t, slopes, batch, seq, cols):
    n_blocks = seq // MOBA_BLOCK
    top_m = min(MOBA_TOPK, n_blocks)
    return pl.pallas_call(
        functools.partial(_moba_kernel, top_m=top_m),
        out_shape=jax.ShapeDtypeStruct((batch * seq, MOBA_HEADS * HEAD_DIM), MXU_DTYPE),
        grid=(batch, MOBA_HEADS, n_blocks),
        in_specs=[pl.BlockSpec((MOBA_BLOCK, HEAD_DIM), lambda b, h, c: (b * n_blocks + c, cols["mq"] + h)),
                  pl.BlockSpec((seq, HEAD_DIM), lambda b, h, c: (b, cols["mk"] + h)),
                  pl.BlockSpec((HEAD_DIM, seq), lambda b, h, c: (b * MOBA_HEADS + h, 0)),
                  pl.BlockSpec((1, 1, LANES), lambda b, h, c: (h, 0, 0))],
        out_specs=pl.BlockSpec((MOBA_BLOCK, HEAD_DIM), lambda b, h, c: (b * n_blocks + c, h)),
        scratch_shapes=[pltpu.VMEM((LANES, HEAD_DIM), F32), pltpu.VMEM((LANES, MOBA_BLOCK), F32)],
        compiler_params=_cparams("parallel", "parallel", "arbitrary"),
        name="moba_attention",
    )(proj, proj, mv_t, slopes)


def _outproj_ln_kernel(a1_ref, a2_ref, x_ref, w1_ref, w2_ref, g_ref, b_ref, o_ref):
    y = DN_ALPHA * x_ref[...] + _dot(a1_ref[...], w1_ref[...]) + _dot(a2_ref[...], w2_ref[...])
    o_ref[...] = _layernorm(y, g_ref[...], b_ref[...])


def _outproj_ln(a1, a2, x, w1, w2, g, b, tm):
    t, d = x.shape
    k1, k2 = a1.shape[1], a2.shape[1]
    const = lambda shape: pl.BlockSpec(shape, lambda i: (0, 0))
    return pl.pallas_call(
        _outproj_ln_kernel,
        out_shape=jax.ShapeDtypeStruct((t, d), F32),
        grid=(t // tm,),
        in_specs=[pl.BlockSpec((tm, k1), lambda i: (i, 0)), pl.BlockSpec((tm, k2), lambda i: (i, 0)),
                  pl.BlockSpec((tm, d), lambda i: (i, 0)),
                  const((k1, d)), const((k2, d)), const((1, d)), const((1, d))],
        out_specs=pl.BlockSpec((tm, d), lambda i: (i, 0)),
        compiler_params=_cparams("parallel"),
        name="out_proj_ln",
    )(a1, a2, x, w1, w2, g, b)


def _xattn_ln_kernel(h_ref, wq_ref, k_ref, v_ref, wo_ref, g_ref, b_ref, o_ref):
    h = h_ref[...]
    q = (_dot(h.astype(MXU_DTYPE), wq_ref[...]) * (XA_DIM ** -0.5)).astype(MXU_DTYPE)
    outs = []
    for hd in range(XA_HEADS):
        cs = slice(hd * XA_DIM, (hd + 1) * XA_DIM)
        s = _dot_nt(q[:, cs], k_ref[:, cs])
        m = jnp.max(s, -1, keepdims=True)
        e = jnp.exp(s - m)
        p = e / jnp.sum(e, -1, keepdims=True)
        outs.append(_dot(p.astype(MXU_DTYPE), v_ref[:, cs]).astype(MXU_DTYPE))
    o = jnp.concatenate(outs, axis=1)
    y = DN_ALPHA * h + _dot(o, wo_ref[...])
    o_ref[...] = _layernorm(y, g_ref[...], b_ref[...])


def _xattn_ln(h, kv, wq, wo, g, b, batch, seq, mem_len, tm):
    t, d = h.shape
    e = XA_HEADS * XA_DIM
    nt = seq // tm
    const = lambda shape: pl.BlockSpec(shape, lambda bi, i: (0, 0))
    return pl.pallas_call(
        _xattn_ln_kernel,
        out_shape=jax.ShapeDtypeStruct((t, d), F32),
        grid=(batch, nt),
        in_specs=[pl.BlockSpec((tm, d), lambda bi, i: (bi * nt + i, 0)),
                  const((d, e)),
                  pl.BlockSpec((mem_len, e), lambda bi, i: (bi, 0)),
                  pl.BlockSpec((mem_len, e), lambda bi, i: (bi, 1)),
                  const((e, d)), const((1, d)), const((1, d))],
        out_specs=pl.BlockSpec((tm, d), lambda bi, i: (bi * nt + i, 0)),
        compiler_params=_cparams("parallel", "parallel"),
        name="xattn_ln",
    )(h, wq, kv, kv, wo, g, b)


def _peer_route_kernel(h_ref, wq_ref, sk_ref, gw_ref, ei_ref):
    hh = pl.program_id(1)
    tm = h_ref.shape[0]
    k_top = PEER_TOPK
    q = _dot(h_ref[...].astype(MXU_DTYPE), wq_ref[...]).astype(MXU_DTYPE)
    key_row = lax.broadcasted_iota(jnp.int32, (PEER_NKEYS, tm), 0).astype(F32)
    tops = []
    for half in range(2):
        s = _dot_nt(sk_ref[0, half], q[:, half * PEER_NKEYS:(half + 1) * PEER_NKEYS])
        tops.append(_top_values_t(s, key_row, k_top))
    (v0, i0), (v1, i1) = tops
    rep = lambda a: jnp.concatenate([jnp.broadcast_to(a[i:i + 1], (k_top, tm)) for i in range(k_top)], axis=0)
    tile = lambda a: jnp.concatenate([a] * k_top, axis=0)
    cand = rep(v0) + tile(v1)
    cidx = rep(i0 * float(PEER_NKEYS)) + tile(i1)
    cand_row = lax.broadcasted_iota(jnp.int32, (k_top * k_top, tm), 0).astype(F32)
    pick_row = lax.broadcasted_iota(jnp.int32, (k_top, tm), 0)

    def pick_expert(k, carry):
        cd, tv, te = carry
        m = jnp.max(cd, 0, keepdims=True)
        j = jnp.min(jnp.where(cd == m, cand_row, float(k_top * k_top)), 0, keepdims=True)
        hit = cand_row == j
        e = jnp.sum(jnp.where(hit, cidx, 0.0), 0, keepdims=True)
        here = pick_row == k
        return jnp.where(hit, NEG_INF, cd), jnp.where(here, m, tv), jnp.where(here, e, te)

    zeros = jnp.zeros((k_top, tm), F32)
    _, tv, te = lax.fori_loop(0, k_top, pick_expert, (cand, zeros, zeros))
    ex = jnp.exp(tv - jnp.max(tv, 0, keepdims=True))
    rows = pl.ds(pl.multiple_of(hh * k_top, k_top), k_top)
    gw_ref[rows, :] = ex / jnp.sum(ex, 0, keepdims=True)
    ei_ref[rows, :] = te.astype(jnp.int32)


def _peer_route(h, wq, sub_keys, tm):
    t, d = h.shape
    return pl.pallas_call(
        _peer_route_kernel,
        out_shape=(jax.ShapeDtypeStruct((PEER_PICKS, t), F32), jax.ShapeDtypeStruct((PEER_PICKS, t), jnp.int32)),
        grid=(t // tm, PEER_HEADS),
        in_specs=[pl.BlockSpec((tm, d), lambda i, hh: (i, 0)),
                  pl.BlockSpec((d, PEER_QDIM), lambda i, hh: (0, hh)),
                  pl.BlockSpec((1, 2, PEER_NKEYS, PEER_QDIM // 2), lambda i, hh: (hh, 0, 0, 0))],
        out_specs=(pl.BlockSpec((PEER_PICKS, tm), lambda i, hh: (0, i)),
                   pl.BlockSpec((PEER_PICKS, tm), lambda i, hh: (0, i))),
        compiler_params=_cparams("parallel", "arbitrary"),
        name="peer_route",
    )(h, wq, sub_keys)


def _peer_expert_kernel(ei_ref, gw_ref, x_ref, uv_ref, o_ref, *scratch):
    bufs, sem_ref = scratch[:PEER_SLOTS], scratch[PEER_SLOTS]
    n_tok = x_ref.shape[0]
    half = PEER_SLAB // 2
    groups = PEER_PICKS // SUBLANES
    lookahead = PEER_SLOTS - 1

    def slab(slot, j):
        return bufs[slot].at[pl.ds(j * PEER_SLAB_PITCH, PEER_SLAB)]

    def issue(tok, slot):
        for j in range(PEER_PICKS):
            pltpu.make_async_copy(uv_ref.at[ei_ref[j, tok]], slab(slot, j), sem_ref.at[slot]).start()

    def wait(slot):
        for j in range(PEER_PICKS):
            pltpu.make_async_copy(uv_ref.at[0], slab(slot, j), sem_ref.at[slot]).wait()

    gw_t = gw_ref[...]
    tok_lane = lax.broadcasted_iota(jnp.int32, gw_t.shape, 1)

    def compute(tok, slot):
        def rows(j0, r):
            return bufs[slot][pl.ds(j0 * PEER_SLAB_PITCH + r, SUBLANES, stride=PEER_SLAB_PITCH), :]

        xs = [x_ref[tok, pl.ds(r, 1), :] for r in range(half)]
        parts = []
        for jg in range(groups):
            acc = rows(jg * SUBLANES, 0) * xs[0]
            for r in range(1, half):
                acc = acc + rows(jg * SUBLANES, r) * xs[r]
            parts.append(jnp.sum(acc, -1, keepdims=True))
        a = jnp.concatenate(parts, axis=0)
        gw_col = jnp.sum(jnp.where(tok_lane == tok, gw_t, 0.0), -1, keepdims=True)
        w = gw_col * jax.nn.gelu(a)
        for r in range(half):
            acc = rows(0, half + r) * w[0:SUBLANES]
            for jg in range(1, groups):
                acc = acc + rows(jg * SUBLANES, half + r) * w[jg * SUBLANES:(jg + 1) * SUBLANES]
            o_ref[tok, pl.ds(r, 1), :] = jnp.sum(acc, 0, keepdims=True)

    def round_of_slots(base, n_issue):
        for s in range(PEER_SLOTS):
            wait(s)
            if s < n_issue:
                issue(base + s + lookahead, (s + lookahead) % PEER_SLOTS)
            compute(base + s, s)

    for s in range(lookahead):
        issue(s, s)
    n_rounds = n_tok // PEER_SLOTS

    def round_body(i, _):
        round_of_slots(i * PEER_SLOTS, PEER_SLOTS)
        return 0

    lax.fori_loop(0, n_rounds - 1, round_body, 0)
    round_of_slots((n_rounds - 1) * PEER_SLOTS, PEER_SLOTS - lookahead)


def _peer_experts(eidx_t, gw_t, x3, uv):
    t = x3.shape[0]
    tt = PEER_TOK_TILE
    half = PEER_SLAB // 2
    return pl.pallas_call(
        _peer_expert_kernel,
        out_shape=jax.ShapeDtypeStruct((t, half, LANES), F32),
        grid=(t // tt,),
        in_specs=[pl.BlockSpec((PEER_PICKS, tt), lambda i: (0, i), memory_space=pltpu.SMEM),
                  pl.BlockSpec((PEER_PICKS, tt), lambda i: (0, i)),
                  pl.BlockSpec((tt, half, LANES), lambda i: (i, 0, 0)),
                  pl.BlockSpec(memory_space=pl.ANY)],
        out_specs=pl.BlockSpec((tt, half, LANES), lambda i: (i, 0, 0)),
        scratch_shapes=[pltpu.VMEM((PEER_PICKS * PEER_SLAB_PITCH, LANES), F32) for _ in range(PEER_SLOTS)]
                       + [pltpu.SemaphoreType.DMA((PEER_SLOTS,))],
        compiler_params=_cparams("arbitrary"),
        name="peer_experts",
    )(eidx_t, gw_t, x3, uv)


def _add_ln_kernel(h_ref, f_ref, g_ref, b_ref, o_ref):
    o_ref[...] = _layernorm(DN_ALPHA * h_ref[...] + f_ref[...], g_ref[...], b_ref[...])


def _add_ln(h, f, g, b, tm):
    t, d = h.shape
    row = pl.BlockSpec((tm, d), lambda i: (i, 0))
    const = pl.BlockSpec((1, d), lambda i: (0, 0))
    return pl.pallas_call(
        _add_ln_kernel,
        out_shape=jax.ShapeDtypeStruct((t, d), F32),
        grid=(t // tm,),
        in_specs=[row, row, const, const],
        out_specs=row,
        compiler_params=_cparams("parallel"),
        name="add_ln",
    )(h, f, g, b)


def _alibi_slopes():
    s = (2.0 ** (-8.0 * (np.arange(N_MIX_HEADS) + 1) / N_MIX_HEADS)).astype(np.float32)
    return s[0::2], s[1::2]


def _mixer(x2, batch, seq, w_in, pe_k, w1_k, w2_k, pe_v, w1_v, w2_v):
    hd = HEAD_DIM
    sizes = [NSA_HEADS * hd] + [NSA_KV_GROUPS * hd] * 6 + [NSA_HEADS * 3] + [MOBA_HEADS * hd] * 3
    offs = np.concatenate([[0], np.cumsum(sizes)])
    sec = {n: w_in[:, offs[i]:offs[i + 1]] for i, n in enumerate(
        ["nq", "kc", "vc", "ks", "vs", "kw", "vw", "gl", "mq", "mk", "mv"])}
    order = ["nq", "ks", "vs", "kw", "vw", "mq", "mk", "mv"]
    cols, at = {}, 0
    for n in order:
        cols[n] = at // LANES
        at += sec[n].shape[1]
    w_main = jnp.concatenate([sec[n] for n in order], axis=1).astype(MXU_DTYPE)
    scale = hd ** -0.5
    col_scale = jnp.concatenate([
        jnp.full((1, sec[n].shape[1]), scale if n in ("nq", "mq") else 1.0, F32) for n in order], axis=1)
    per_group = NSA_HPG * 3
    gl_cols = [jnp.pad(sec["gl"][:, g * per_group:(g + 1) * per_group], ((0, 0), (0, LANES - per_group)))
               for g in range(NSA_KV_GROUPS)]
    w_aux = jnp.concatenate([sec["kc"], sec["vc"]] + gl_cols, axis=1).astype(MXU_DTYPE)
    cols["gl_aux"] = (2 * NSA_KV_GROUPS * hd) // LANES

    xb = x2.astype(MXU_DTYPE)
    tm = 512 if x2.shape[0] % 512 == 0 else 256
    proj = _matmul(xb, w_main, col_scale, MXU_DTYPE, tm, 512, "in_proj")
    aux = _matmul(xb, w_aux, jnp.ones((1, w_aux.shape[1]), F32), F32, tm, 256, "in_proj_aux")

    ns = seq // NSA_CMP_STRIDE

    def strips(col0):
        raw = aux[:, col0:col0 + NSA_KV_GROUPS * hd].reshape(batch, seq, NSA_KV_GROUPS, hd)
        return raw.transpose(0, 2, 1, 3).reshape(batch * NSA_KV_GROUPS, ns, NSA_CMP_STRIDE * hd)

    kc = _compress(strips(0), pe_k, w1_k, w2_k)
    vc = _compress(strips(NSA_KV_GROUPS * hd), pe_v, w1_v, w2_v)

    def values_t(name, heads):
        c0 = cols[name] * LANES
        v = proj[:, c0:c0 + heads * hd].reshape(batch, seq, heads * hd)
        return v.transpose(0, 2, 1).reshape(batch * heads * hd, seq)

    slope_n, slope_m = _alibi_slopes()
    sn = np.repeat(slope_n.reshape(NSA_KV_GROUPS, NSA_HPG), Q_BLK, axis=1)
    sn_rows = jnp.asarray(np.broadcast_to(sn[:, :, None], sn.shape + (LANES,)).copy())
    sn_lanes = jnp.asarray(sn[:, None, :].copy())
    sm = jnp.asarray(np.broadcast_to(slope_m[:, None, None], (MOBA_HEADS, 1, LANES)).copy())

    o_nsa = _nsa(proj, aux, kc, vc, values_t("vs", NSA_KV_GROUPS), sn_rows, sn_lanes, batch, seq, cols)
    o_moba = _moba(proj, values_t("mv", MOBA_HEADS), sm, batch, seq, cols)
    return o_nsa, o_moba


def _memory_xattn_ln(h, mem2, batch, seq, wq, wkv, wo, g, b):
    mem_len = mem2.shape[0] // batch
    e2 = wkv.shape[1]
    kv = _matmul(mem2.astype(MXU_DTYPE), wkv.astype(MXU_DTYPE), jnp.ones((1, e2), F32), MXU_DTYPE,
                 mem_len, e2 // 2, "xattn_kv")
    return _xattn_ln(h, kv, wq.astype(MXU_DTYPE), wo.astype(MXU_DTYPE), g, b, batch, seq, mem_len, 256)


def _peer_ln(h, wq, sub_keys, exp_u, exp_v, g, b):
    t, d = h.shape
    gw_t, eidx_t = _peer_route(h, wq.astype(MXU_DTYPE), sub_keys.astype(MXU_DTYPE), 256)
    half = PEER_SLAB // 2
    n_exp = exp_u.shape[0]
    uv = jnp.concatenate([exp_u.reshape(n_exp, half, LANES), exp_v.reshape(n_exp, half, LANES)], axis=1)
    f = _peer_experts(eidx_t, gw_t, h.reshape(t, half, LANES), uv).reshape(t, d)
    return _add_ln(h, f, g, b, 256)


def kernel(x, mem, w_in, cmp_pe_k, cmp_w1_k, cmp_w2_k, cmp_pe_v, cmp_w1_v, cmp_w2_v, w_out, ln1_g, ln1_b,
           xa_wq, xa_wkv, xa_wo, ln2_g, ln2_b, peer_wq, peer_subkeys, peer_u, peer_v, ln3_g, ln3_b):
    batch, seq, d = x.shape
    assert seq % MOBA_STEP == 0 and WIN_SPAN <= seq <= NSA_SEL_BLOCK * LANES
    assert d == PEER_SLAB // 2 * LANES and w_in.shape[0] == DEPTH
    row = lambda v: v.reshape(1, d)
    h = x.reshape(batch * seq, d)
    mem2 = mem.reshape(-1, d)
    for l in range(DEPTH):
        o_nsa, o_moba = _mixer(h, batch, seq, w_in[l], cmp_pe_k[l], cmp_w1_k[l], cmp_w2_k[l],
                               cmp_pe_v[l], cmp_w1_v[l], cmp_w2_v[l])
        wo = w_out[l].astype(MXU_DTYPE)
        k1 = o_nsa.shape[1]
        h = _outproj_ln(o_nsa, o_moba, h, wo[:k1], wo[k1:], row(ln1_g[l]), row(ln1_b[l]), 256)
        h = _memory_xattn_ln(h, mem2, batch, seq, xa_wq[l], xa_wkv[l], xa_wo[l], row(ln2_g[l]), row(ln2_b[l]))
        h = _peer_ln(h, peer_wq[l], peer_subkeys[l], peer_u[l], peer_v[l], row(ln3_g[l]), row(ln3_b[l]))
    return h.reshape(batch, seq, d)
```

```python
import functools

import numpy as np
import jax
import jax.numpy as jnp
from jax import lax
from jax.experimental import pallas as pl
from jax.experimental.pallas import tpu as pltpu

F32 = jnp.float32
MXU_DTYPE = jnp.bfloat16
NEG_INF = float("-inf")

LANES = 128
SUBLANES = 8
VMEM_LIMIT = 48 * 1024 * 1024

HEAD_DIM = 128
N_MIX_HEADS = 16
NSA_HEADS = 8
NSA_KV_GROUPS = 2
NSA_HPG = NSA_HEADS // NSA_KV_GROUPS
NSA_CMP_STRIDE = 16
NSA_CMP_LEN = 32
NSA_SEL_BLOCK = 64
NSA_TOPN = 16
NSA_WINDOW = 512
NSA_FORCE_BONUS = 1.0e4
MOBA_HEADS = 8
MOBA_BLOCK = 256
MOBA_TOPK = 3
Q_BLK = 64
XA_HEADS = 4
XA_DIM = 128
PEER_HEADS = 8
PEER_NKEYS = 128
PEER_QDIM = 256
PEER_TOPK = 16
DEPTH = 1
DN_ALPHA = (2 * DEPTH) ** 0.25
LN_EPS = 1e-5

NSA_ROWS = NSA_HPG * Q_BLK
WIN_SPAN = NSA_WINDOW + 2 * Q_BLK
SEL_STEP_BLOCKS = SUBLANES
SEL_STEP = SEL_STEP_BLOCKS * NSA_SEL_BLOCK
MOBA_STEP_BLOCKS = 4
MOBA_STEP = MOBA_STEP_BLOCKS * MOBA_BLOCK
PEER_PICKS = PEER_HEADS * PEER_TOPK
PEER_SLAB = 16
PEER_SLAB_PITCH = 24
PEER_SLOTS = 4
PEER_TOK_TILE = LANES


def _cparams(*sem):
    return pltpu.CompilerParams(dimension_semantics=sem, vmem_limit_bytes=VMEM_LIMIT)


def _dot(a, b):
    return jnp.dot(a, b, preferred_element_type=F32)


def _dot_nt(a, b):
    return lax.dot_general(a, b, (((1,), (1,)), ((), ())), preferred_element_type=F32)


def _split3(x):
    hi = x.astype(MXU_DTYPE)
    r1 = x - hi.astype(F32)
    mid = r1.astype(MXU_DTYPE)
    lo = (r1 - mid.astype(F32)).astype(MXU_DTYPE)
    return hi, mid, lo


def _masked_softmax(s, mask):
    s = jnp.where(mask, s, NEG_INF)
    m = jnp.max(s, -1, keepdims=True)
    m = jnp.where(jnp.isfinite(m), m, 0.0)
    e = jnp.where(mask, jnp.exp(s - m), 0.0)
    d = jnp.sum(e, -1, keepdims=True)
    return e / jnp.where(d > 0, d, 1.0)


def _online_step_t(carry, s, row_shift, vt):
    m_i, l_i, acc = carry
    m_new = jnp.maximum(m_i, jnp.max(s, 0, keepdims=True) - row_shift)
    m_safe = jnp.where(m_new == NEG_INF, 0.0, m_new)
    alpha = jnp.exp(m_i - m_safe)
    p = jnp.exp(s - (m_safe + row_shift))
    l_new = alpha * l_i + jnp.sum(p, 0, keepdims=True)
    acc_new = alpha * acc + _dot(vt, p.astype(MXU_DTYPE))
    return m_new, l_new, acc_new


def _flash_init(n_q):
    return (jnp.full((1, n_q), NEG_INF, F32), jnp.zeros((1, n_q), F32), jnp.zeros((HEAD_DIM, n_q), F32))


def _layernorm(y, g, b):
    mu = jnp.mean(y, -1, keepdims=True)
    var = jnp.mean(jnp.square(y - mu), -1, keepdims=True)
    return (y - mu) * lax.rsqrt(var + LN_EPS) * g + b


def _top_select_t(score, row, n_pick):
    sentinel = float(score.shape[0])

    def body(_, carry):
        s, sel = carry
        m = jnp.max(s, 0, keepdims=True)
        idx = jnp.min(jnp.where(s == m, row, sentinel), 0, keepdims=True)
        hit = row == idx
        return jnp.where(hit, NEG_INF, s), jnp.where(hit, 1.0, sel)

    _, sel = lax.fori_loop(0, n_pick, body, (score, jnp.zeros_like(score)))
    return sel


def _top_values_t(score, row, n_pick):
    sentinel = float(score.shape[0])
    n = score.shape[1]
    out_row = lax.broadcasted_iota(jnp.int32, (n_pick, n), 0)

    def body(k, carry):
        s, vals, idxs = carry
        m = jnp.max(s, 0, keepdims=True)
        idx = jnp.min(jnp.where(s == m, row, sentinel), 0, keepdims=True)
        here = out_row == k
        return jnp.where(row == idx, NEG_INF, s), jnp.where(here, m, vals), jnp.where(here, idx, idxs)

    zeros = jnp.zeros((n_pick, n), F32)
    _, vals, idxs = lax.fori_loop(0, n_pick, body, (score, zeros, zeros))
    return vals, idxs


def _mm_kernel(a_ref, b_ref, s_ref, o_ref):
    acc = _dot(a_ref[...], b_ref[...])
    o_ref[...] = (acc * s_ref[...]).astype(o_ref.dtype)


def _matmul(a, b, col_scale, out_dtype, tm, tn, name):
    m, k = a.shape
    n = b.shape[1]
    return pl.pallas_call(
        _mm_kernel,
        out_shape=jax.ShapeDtypeStruct((m, n), out_dtype),
        grid=(m // tm, n // tn),
        in_specs=[pl.BlockSpec((tm, k), lambda i, j: (i, 0)),
                  pl.BlockSpec((k, tn), lambda i, j: (0, j)),
                  pl.BlockSpec((1, tn), lambda i, j: (0, j))],
        out_specs=pl.BlockSpec((tm, tn), lambda i, j: (i, j)),
        compiler_params=_cparams("parallel", "parallel"),
        name=name,
    )(a, b, col_scale)


def _compress_kernel(r_ref, pelo_ref, pehi_ref, w1lo_ref, w1hi_ref, w2_ref, o_ref):
    r = r_ref[0]
    ns = r.shape[0]
    lo = _dot((r + pelo_ref[...]).astype(MXU_DTYPE), w1lo_ref[...])
    hi = _dot((r + pehi_ref[...]).astype(MXU_DTYPE), w1hi_ref[...])
    hid = jax.nn.gelu(lo + pltpu.roll(hi, ns - 1, 0))
    o_ref[0] = _dot(hid.astype(MXU_DTYPE), w2_ref[...]).astype(o_ref.dtype)


def _compress(strips, pe, w1, w2):
    bg, ns, width = strips.shape
    half = NSA_CMP_LEN // 2
    pelo = pe[:half].reshape(1, width)
    pehi = pe[half:].reshape(1, width)
    w1lo = w1[:half].reshape(width, HEAD_DIM).astype(MXU_DTYPE)
    w1hi = w1[half:].reshape(width, HEAD_DIM).astype(MXU_DTYPE)
    const = lambda shape: pl.BlockSpec(shape, lambda i: (0,) * len(shape))
    return pl.pallas_call(
        _compress_kernel,
        out_shape=jax.ShapeDtypeStruct((bg, ns, HEAD_DIM), MXU_DTYPE),
        grid=(bg,),
        in_specs=[pl.BlockSpec((1, ns, width), lambda i: (i, 0, 0)),
                  const((1, width)), const((1, width)),
                  const((width, HEAD_DIM)), const((width, HEAD_DIM)), const((HEAD_DIM, HEAD_DIM))],
        out_specs=pl.BlockSpec((1, ns, HEAD_DIM), lambda i: (i, 0, 0)),
        compiler_params=_cparams("parallel"),
        name="nsa_compress",
    )(strips, pelo, pehi, w1lo, w1hi, w2.astype(MXU_DTYPE))


def _nsa_kernel(q_ref, kc_ref, vc_ref, ks_ref, vst_ref, kw_ref, vw_ref, gl_ref, slope_ref, slope_t_ref,
                o_ref, selt_ref, *, top_n):
    c = pl.program_id(2)
    q0 = c * Q_BLK
    rows = NSA_ROWS
    q = q_ref[...]
    qh = jnp.concatenate([q[:, h * HEAD_DIM:(h + 1) * HEAD_DIM] for h in range(NSA_HPG)], axis=0)
    slope = slope_ref[0][:, 0:1]
    slope_t = slope_t_ref[0]

    def tpos(width):
        return q0 + (lax.broadcasted_iota(jnp.int32, (rows, width), 0) & (Q_BLK - 1))

    kc = kc_ref[0]
    ns = kc.shape[0]
    cend = lax.broadcasted_iota(jnp.int32, (rows, ns), 1) * NSA_CMP_STRIDE + (NSA_CMP_LEN - 1)
    t_c = tpos(ns)
    s_c = _dot_nt(qh, kc) - slope * (t_c - cend).astype(F32)
    p_c = _masked_softmax(s_c, cend <= t_c)
    o_c = _dot(p_c.astype(MXU_DTYPE), vc_ref[0])

    p_sum = p_c[0:Q_BLK]
    for h in range(1, NSA_HPG):
        p_sum = p_sum + p_c[h * Q_BLK:(h + 1) * Q_BLK]
    p_two = jnp.concatenate([p_sum, p_sum], axis=0)
    ratio = NSA_SEL_BLOCK // NSA_CMP_STRIDE
    gj = lax.broadcasted_iota(jnp.int32, (LANES, ns), 0) * ratio
    gi = lax.broadcasted_iota(jnp.int32, (LANES, ns), 1)
    gather01 = jnp.where((gi >= gj - 1) & (gi <= gj + ratio - 1), 1.0, 0.0).astype(MXU_DTYPE)
    hi, mid, lo = _split3(p_two)
    imp_t = _dot_nt(gather01, hi) + _dot_nt(gather01, mid) + _dot_nt(gather01, lo)
    jrow_i = lax.broadcasted_iota(jnp.int32, (LANES, LANES), 0)
    forced = (jrow_i == 0) | (jrow_i == c) | (jrow_i == c - 1)
    valid = jrow_i <= c
    score_t = jnp.where(valid, imp_t + jnp.where(forced, NSA_FORCE_BONUS, 0.0), NEG_INF)
    sel_t = jnp.where(valid, _top_select_t(score_t, jrow_i.astype(F32), top_n), 0.0)
    selt_ref[...] = jnp.concatenate([sel_t, sel_t], axis=1)

    off = (lax.broadcasted_iota(jnp.int32, (SEL_STEP, rows), 0)
           - (lax.broadcasted_iota(jnp.int32, (SEL_STEP, rows), 1) & (Q_BLK - 1)))
    bias_t = slope_t * (-off).astype(F32)

    def step_picks(kb):
        return selt_ref[pl.ds(pl.multiple_of(kb * SEL_STEP_BLOCKS, SEL_STEP_BLOCKS), SEL_STEP_BLOCKS), :]

    def sel_step(kb, carry, causal):
        k0 = pl.multiple_of(kb * SEL_STEP, SEL_STEP)
        s = _dot_nt(ks_ref[pl.ds(k0, SEL_STEP), :], qh) - bias_t
        picks = step_picks(kb)
        pieces = []
        for i in range(SEL_STEP_BLOCKS):
            r = slice(i * NSA_SEL_BLOCK, (i + 1) * NSA_SEL_BLOCK)
            ok = picks[i:i + 1, :] > 0.5
            if causal:
                ok = ok & (off[r] <= q0 - k0)
            pieces.append(jnp.where(ok, s[r], NEG_INF))
        s = jnp.concatenate(pieces, axis=0)
        shift = slope_t * (q0 - k0).astype(F32)
        return _online_step_t(carry, s, shift, vst_ref[:, pl.ds(k0, SEL_STEP)])

    def maybe_step(kb, carry):
        any_pick = jnp.max(jnp.max(step_picks(kb), 1, keepdims=True), 0, keepdims=True)[0, 0] > 0.5
        return lax.cond(any_pick, lambda cr: sel_step(kb, cr, causal=False), lambda cr: cr, carry)

    last = c // SEL_STEP_BLOCKS
    carry = lax.fori_loop(0, last, maybe_step, _flash_init(rows))
    _, l_s, acc_s = sel_step(last, carry, causal=True)
    o_s = (acc_s / jnp.where(l_s > 0, l_s, 1.0)).T

    w0 = pl.multiple_of(jnp.maximum(q0 - (WIN_SPAN - Q_BLK), 0), Q_BLK)
    t_w = tpos(WIN_SPAN)
    dist_w = t_w - (w0 + lax.broadcasted_iota(jnp.int32, (rows, WIN_SPAN), 1))
    s_w = _dot_nt(qh, kw_ref[pl.ds(w0, WIN_SPAN), :]) - slope * dist_w.astype(F32)
    p_w = _masked_softmax(s_w, (dist_w >= 0) & (dist_w < NSA_WINDOW))
    o_w = _dot(p_w.astype(MXU_DTYPE), vw_ref[pl.ds(w0, WIN_SPAN), :])

    gates = 1.0 / (1.0 + jnp.exp(-gl_ref[...]))
    for h in range(NSA_HPG):
        r = slice(h * Q_BLK, (h + 1) * Q_BLK)
        o = (gates[:, 3 * h:3 * h + 1] * o_c[r] + gates[:, 3 * h + 1:3 * h + 2] * o_s[r]
             + gates[:, 3 * h + 2:3 * h + 3] * o_w[r])
        o_ref[:, h * HEAD_DIM:(h + 1) * HEAD_DIM] = o.astype(o_ref.dtype)


def _nsa(proj, aux, kc, vc, vs_t, slopes, slopes_t, batch, seq, cols):
    n_chunks = seq // Q_BLK
    ns = kc.shape[1]
    gw = NSA_HPG * HEAD_DIM
    top_n = min(NSA_TOPN, seq // NSA_SEL_BLOCK)
    kv = lambda base: pl.BlockSpec((seq, HEAD_DIM), lambda b, g, c: (b, base + g))
    cmp_spec = pl.BlockSpec((1, ns, HEAD_DIM), lambda b, g, c: (b * NSA_KV_GROUPS + g, 0, 0))
    return pl.pallas_call(
        functools.partial(_nsa_kernel, top_n=top_n),
        out_shape=jax.ShapeDtypeStruct((batch * seq, NSA_HEADS * HEAD_DIM), MXU_DTYPE),
        grid=(batch, NSA_KV_GROUPS, n_chunks),
        in_specs=[pl.BlockSpec((Q_BLK, gw), lambda b, g, c: (b * n_chunks + c, g)),
                  cmp_spec, cmp_spec,
                  kv(cols["ks"]),
                  pl.BlockSpec((HEAD_DIM, seq), lambda b, g, c: (b * NSA_KV_GROUPS + g, 0)),
                  kv(cols["kw"]), kv(cols["vw"]),
                  pl.BlockSpec((Q_BLK, LANES), lambda b, g, c: (b * n_chunks + c, cols["gl_aux"] + g)),
                  pl.BlockSpec((1, NSA_ROWS, LANES), lambda b, g, c: (g, 0, 0)),
                  pl.BlockSpec((1, 1, NSA_ROWS), lambda b, g, c: (g, 0, 0))],
        out_specs=pl.BlockSpec((Q_BLK, gw), lambda b, g, c: (b * n_chunks + c, g)),
        scratch_shapes=[pltpu.VMEM((LANES, NSA_ROWS), F32)],
        compiler_params=_cparams("parallel", "parallel", "arbitrary"),
        name="nsa_attention",
    )(proj, kc, vc, proj, vs_t, proj, proj, aux, slopes, slopes_t)


def _moba_kernel(q_ref, k_ref, vt_ref, slope_ref, o_ref, kmean_ref, selt_ref, *, top_m):
    cb = pl.program_id(2)
    seq = k_ref.shape[0]
    n_blocks = seq // MOBA_BLOCK
    nq = MOBA_BLOCK

    @pl.when(cb == 0)
    def _():
        kmean_ref[...] = jnp.zeros_like(kmean_ref)
        kf = k_ref[...].astype(F32).reshape(n_blocks, MOBA_BLOCK, HEAD_DIM)
        kmean_ref[0:n_blocks, :] = jnp.sum(kf, axis=1) * (1.0 / MOBA_BLOCK)

    q = q_ref[...]
    slope = slope_ref[0][:, 0:1]
    brow_i = lax.broadcasted_iota(jnp.int32, (LANES, nq), 0)
    gate_t = jnp.where(brow_i < cb, _dot_nt(kmean_ref[...].astype(MXU_DTYPE), q), NEG_INF)
    selt_ref[...] = jnp.where(brow_i < cb, _top_select_t(gate_t, brow_i.astype(F32), top_m), 0.0)

    off = (lax.broadcasted_iota(jnp.int32, (MOBA_STEP, nq), 0)
           - lax.broadcasted_iota(jnp.int32, (MOBA_STEP, nq), 1))
    bias_t = slope * (-off).astype(F32)
    q0 = cb * MOBA_BLOCK

    def past_step(st, carry):
        k0 = pl.multiple_of(st * MOBA_STEP, MOBA_STEP)
        s = _dot_nt(k_ref[pl.ds(k0, MOBA_STEP), :], q) - bias_t
        pair = selt_ref[pl.ds(pl.multiple_of((st // 2) * SUBLANES, SUBLANES), SUBLANES), :]
        picks = jnp.where(st % 2 == 0, pair[0:MOBA_STEP_BLOCKS], pair[MOBA_STEP_BLOCKS:2 * MOBA_STEP_BLOCKS])
        pieces = [jnp.where(picks[i:i + 1, :] > 0.5, s[i * MOBA_BLOCK:(i + 1) * MOBA_BLOCK], NEG_INF)
                  for i in range(MOBA_STEP_BLOCKS)]
        shift = slope * (q0 - k0).astype(F32)
        return _online_step_t(carry, jnp.concatenate(pieces, axis=0), shift, vt_ref[:, pl.ds(k0, MOBA_STEP)])

    n_steps = (cb + MOBA_STEP_BLOCKS - 1) // MOBA_STEP_BLOCKS
    carry = lax.fori_loop(0, n_steps, past_step, _flash_init(nq))
    kc0 = pl.multiple_of(q0, MOBA_BLOCK)
    s = _dot_nt(k_ref[pl.ds(kc0, MOBA_BLOCK), :], q) - bias_t[0:MOBA_BLOCK]
    s = jnp.where(off[0:MOBA_BLOCK] <= 0, s, NEG_INF)
    _, l, acc = _online_step_t(carry, s, jnp.zeros((1, 1), F32), vt_ref[:, pl.ds(kc0, MOBA_BLOCK)])
    o_ref[...] = (acc / jnp.where(l > 0, l, 1.0)).T.astype(o_ref.dtype)


def _moba(proj, mv_t, slopes, batch, seq, cols):
    n_blocks = seq // MOBA_BLOCK
    top_m = min(MOBA_TOPK, n_blocks)
    return pl.pallas_call(
        functools.partial(_moba_kernel, top_m=top_m),
        out_shape=jax.ShapeDtypeStruct((batch * seq, MOBA_HEADS * HEAD_DIM), MXU_DTYPE),
        grid=(batch, MOBA_HEADS, n_blocks),
        in_specs=[pl.BlockSpec((MOBA_BLOCK, HEAD_DIM), lambda b, h, c: (b * n_blocks + c, cols["mq"] + h)),
                  pl.BlockSpec((seq, HEAD_DIM), lambda b, h, c: (b, cols["mk"] + h)),
                  pl.BlockSpec((HEAD_DIM, seq), lambda b, h, c: (b * MOBA_HEADS + h, 0)),
                  pl.BlockSpec((1, 1, LANES), lambda b, h, c: (h, 0, 0))],
        out_specs=pl.BlockSpec((MOBA_BLOCK, HEAD_DIM), lambda b, h, c: (b * n_blocks + c, h)),
        scratch_shapes=[pltpu.VMEM((LANES, HEAD_DIM), F32), pltpu.VMEM((LANES, MOBA_BLOCK), F32)],
        compiler_params=_cparams("parallel", "parallel", "arbitrary"),
        name="moba_attention",
    )(proj, proj, mv_t, slopes)


def _outproj_ln_kernel(a1_ref, a2_ref, x_ref, w1_ref, w2_ref, g_ref, b_ref, o_ref):
    y = DN_ALPHA * x_ref[...] + _dot(a1_ref[...], w1_ref[...]) + _dot(a2_ref[...], w2_ref[...])
    o_ref[...] = _layernorm(y, g_ref[...], b_ref[...])


def _outproj_ln(a1, a2, x, w1, w2, g, b, tm):
    t, d = x.shape
    k1, k2 = a1.shape[1], a2.shape[1]
    const = lambda shape: pl.BlockSpec(shape, lambda i: (0, 0))
    return pl.pallas_call(
        _outproj_ln_kernel,
        out_shape=jax.ShapeDtypeStruct((t, d), F32),
        grid=(t // tm,),
        in_specs=[pl.BlockSpec((tm, k1), lambda i: (i, 0)), pl.BlockSpec((tm, k2), lambda i: (i, 0)),
                  pl.BlockSpec((tm, d), lambda i: (i, 0)),
                  const((k1, d)), const((k2, d)), const((1, d)), const((1, d))],
        out_specs=pl.BlockSpec((tm, d), lambda i: (i, 0)),
        compiler_params=_cparams("parallel"),
        name="out_proj_ln",
    )(a1, a2, x, w1, w2, g, b)


def _xattn_ln_kernel(h_ref, wq_ref, k_ref, v_ref, wo_ref, g_ref, b_ref, o_ref):
    h = h_ref[...]
    q = (_dot(h.astype(MXU_DTYPE), wq_ref[...]) * (XA_DIM ** -0.5)).astype(MXU_DTYPE)
    outs = []
    for hd in range(XA_HEADS):
        cs = slice(hd * XA_DIM, (hd + 1) * XA_DIM)
        s = _dot_nt(q[:, cs], k_ref[:, cs])
        m = jnp.max(s, -1, keepdims=True)
        e = jnp.exp(s - m)
        p = e / jnp.sum(e, -1, keepdims=True)
        outs.append(_dot(p.astype(MXU_DTYPE), v_ref[:, cs]).astype(MXU_DTYPE))
    o = jnp.concatenate(outs, axis=1)
    y = DN_ALPHA * h + _dot(o, wo_ref[...])
    o_ref[...] = _layernorm(y, g_ref[...], b_ref[...])


def _xattn_ln(h, kv, wq, wo, g, b, batch, seq, mem_len, tm):
    t, d = h.shape
    e = XA_HEADS * XA_DIM
    nt = seq // tm
    const = lambda shape: pl.BlockSpec(shape, lambda bi, i: (0, 0))
    return pl.pallas_call(
        _xattn_ln_kernel,
        out_shape=jax.ShapeDtypeStruct((t, d), F32),
        grid=(batch, nt),
        in_specs=[pl.BlockSpec((tm, d), lambda bi, i: (bi * nt + i, 0)),
                  const((d, e)),
                  pl.BlockSpec((mem_len, e), lambda bi, i: (bi, 0)),
                  pl.BlockSpec((mem_len, e), lambda bi, i: (bi, 1)),
                  const((e, d)), const((1, d)), const((1, d))],
        out_specs=pl.BlockSpec((tm, d), lambda bi, i: (bi * nt + i, 0)),
        compiler_params=_cparams("parallel", "parallel"),
        name="xattn_ln",
    )(h, wq, kv, kv, wo, g, b)


def _peer_route_kernel(h_ref, wq_ref, sk_ref, gw_ref, ei_ref):
    hh = pl.program_id(1)
    tm = h_ref.shape[0]
    k_top = PEER_TOPK
    q = _dot(h_ref[...].astype(MXU_DTYPE), wq_ref[...]).astype(MXU_DTYPE)
    key_row = lax.broadcasted_iota(jnp.int32, (PEER_NKEYS, tm), 0).astype(F32)
    tops = []
    for half in range(2):
        s = _dot_nt(sk_ref[0, half], q[:, half * PEER_NKEYS:(half + 1) * PEER_NKEYS])
        tops.append(_top_values_t(s, key_row, k_top))
    (v0, i0), (v1, i1) = tops

    def grid(a, b):
        first = [a[0:1] + b]
        mid = [a[i:i + 1] + b[0:SUBLANES] for i in range(1, SUBLANES)]
        return jnp.concatenate(first + mid + [a[SUBLANES:] + b[0:1]], axis=0)

    cand = grid(v0, v1)
    cidx = grid(i0 * float(PEER_NKEYS), i1)
    n_cand = cand.shape[0]
    cand_row = lax.broadcasted_iota(jnp.int32, (n_cand, tm), 0).astype(F32)
    pick_row = lax.broadcasted_iota(jnp.int32, (k_top, tm), 0)

    def pick_expert(k, carry):
        cd, tv, te = carry
        m = jnp.max(cd, 0, keepdims=True)
        j = jnp.min(jnp.where(cd == m, cand_row, float(n_cand)), 0, keepdims=True)
        hit = cand_row == j
        e = jnp.sum(jnp.where(hit, cidx, 0.0), 0, keepdims=True)
        here = pick_row == k
        return jnp.where(hit, NEG_INF, cd), jnp.where(here, m, tv), jnp.where(here, e, te)

    zeros = jnp.zeros((k_top, tm), F32)
    _, tv, te = lax.fori_loop(0, k_top, pick_expert, (cand, zeros, zeros))
    ex = jnp.exp(tv - jnp.max(tv, 0, keepdims=True))
    rows = pl.ds(pl.multiple_of(hh * k_top, k_top), k_top)
    gw_ref[rows, :] = ex / jnp.sum(ex, 0, keepdims=True)
    ei_ref[rows, :] = te.astype(jnp.int32)


def _peer_route(h, wq, sub_keys, tm):
    t, d = h.shape
    return pl.pallas_call(
        _peer_route_kernel,
        out_shape=(jax.ShapeDtypeStruct((PEER_PICKS, t), F32), jax.ShapeDtypeStruct((PEER_PICKS, t), jnp.int32)),
        grid=(t // tm, PEER_HEADS),
        in_specs=[pl.BlockSpec((tm, d), lambda i, hh: (i, 0)),
                  pl.BlockSpec((d, PEER_QDIM), lambda i, hh: (0, hh)),
                  pl.BlockSpec((1, 2, PEER_NKEYS, PEER_QDIM // 2), lambda i, hh: (hh, 0, 0, 0))],
        out_specs=(pl.BlockSpec((PEER_PICKS, tm), lambda i, hh: (0, i)),
                   pl.BlockSpec((PEER_PICKS, tm), lambda i, hh: (0, i))),
        compiler_params=_cparams("parallel", "arbitrary"),
        name="peer_route",
    )(h, wq, sub_keys)


def _peer_expert_kernel(ei_ref, gw_ref, x_ref, uv_ref, o_ref, *scratch):
    bufs, sem_ref = scratch[:PEER_SLOTS], scratch[PEER_SLOTS]
    n_tok = x_ref.shape[0]
    half = PEER_SLAB // 2
    groups = PEER_PICKS // SUBLANES
    lookahead = PEER_SLOTS - 1

    def slab(slot, j):
        return bufs[slot].at[pl.ds(j * PEER_SLAB_PITCH, PEER_SLAB)]

    def issue(tok, slot):
        for j in range(PEER_PICKS):
            pltpu.make_async_copy(uv_ref.at[ei_ref[j, tok]], slab(slot, j), sem_ref.at[slot]).start()

    def wait(slot):
        for j in range(PEER_PICKS):
            pltpu.make_async_copy(uv_ref.at[0], slab(slot, j), sem_ref.at[slot]).wait()

    gw_t = gw_ref[...]
    tok_lane = lax.broadcasted_iota(jnp.int32, gw_t.shape, 1)

    def compute(tok, slot):
        def words(j0, r):
            w = bufs[slot][pl.ds(j0 * PEER_SLAB_PITCH + r, SUBLANES, stride=PEER_SLAB_PITCH), :]
            return (lax.bitcast_convert_type(w << 16, F32),
                    lax.bitcast_convert_type(w & jnp.uint32(0xFFFF0000), F32))

        xs = [x_ref[tok, pl.ds(r, 1), :] for r in range(2 * half)]
        parts = []
        for jg in range(groups):
            acc = None
            for r in range(half):
                lo, hi = words(jg * SUBLANES, r)
                term = lo * xs[r] + hi * xs[half + r]
                acc = term if acc is None else acc + term
            parts.append(jnp.sum(acc, -1, keepdims=True))
        a = jnp.concatenate(parts, axis=0)
        gw_col = jnp.sum(jnp.where(tok_lane == tok, gw_t, 0.0), -1, keepdims=True)
        w = gw_col * jax.nn.gelu(a)
        ws = [w[jg * SUBLANES:(jg + 1) * SUBLANES] for jg in range(groups)]
        for r in range(half):
            acc_lo = acc_hi = None
            for jg in range(groups):
                lo, hi = words(jg * SUBLANES, half + r)
                acc_lo = lo * ws[jg] if acc_lo is None else acc_lo + lo * ws[jg]
                acc_hi = hi * ws[jg] if acc_hi is None else acc_hi + hi * ws[jg]
            o_ref[tok, pl.ds(r, 1), :] = jnp.sum(acc_lo, 0, keepdims=True)
            o_ref[tok, pl.ds(half + r, 1), :] = jnp.sum(acc_hi, 0, keepdims=True)

    def round_of_slots(base, n_issue):
        for s in range(PEER_SLOTS):
            wait(s)
            if s < n_issue:
                issue(base + s + lookahead, (s + lookahead) % PEER_SLOTS)
            compute(base + s, s)

    for s in range(lookahead):
        issue(s, s)
    n_rounds = n_tok // PEER_SLOTS

    def round_body(i, _):
        round_of_slots(i * PEER_SLOTS, PEER_SLOTS)
        return 0

    lax.fori_loop(0, n_rounds - 1, round_body, 0)
    round_of_slots((n_rounds - 1) * PEER_SLOTS, PEER_SLOTS - lookahead)


def _peer_experts(eidx_t, gw_t, x3, uv):
    t = x3.shape[0]
    tt = PEER_TOK_TILE
    return pl.pallas_call(
        _peer_expert_kernel,
        out_shape=jax.ShapeDtypeStruct((t, PEER_SLAB, LANES), F32),
        grid=(t // tt,),
        in_specs=[pl.BlockSpec((PEER_PICKS, tt), lambda i: (0, i), memory_space=pltpu.SMEM),
                  pl.BlockSpec((PEER_PICKS, tt), lambda i: (0, i)),
                  pl.BlockSpec((tt, PEER_SLAB, LANES), lambda i: (i, 0, 0)),
                  pl.BlockSpec(memory_space=pl.ANY)],
        out_specs=pl.BlockSpec((tt, PEER_SLAB, LANES), lambda i: (i, 0, 0)),
        scratch_shapes=[pltpu.VMEM((PEER_PICKS * PEER_SLAB_PITCH, LANES), jnp.uint32) for _ in range(PEER_SLOTS)]
                       + [pltpu.SemaphoreType.DMA((PEER_SLOTS,))],
        compiler_params=_cparams("arbitrary"),
        name="peer_experts",
    )(eidx_t, gw_t, x3, uv)


def _add_ln_kernel(h_ref, f_ref, g_ref, b_ref, o_ref):
    o_ref[...] = _layernorm(DN_ALPHA * h_ref[...] + f_ref[...], g_ref[...], b_ref[...])


def _add_ln(h, f, g, b, tm):
    t, d = h.shape
    row = pl.BlockSpec((tm, d), lambda i: (i, 0))
    const = pl.BlockSpec((1, d), lambda i: (0, 0))
    return pl.pallas_call(
        _add_ln_kernel,
        out_shape=jax.ShapeDtypeStruct((t, d), F32),
        grid=(t // tm,),
        in_specs=[row, row, const, const],
        out_specs=row,
        compiler_params=_cparams("parallel"),
        name="add_ln",
    )(h, f, g, b)


def _alibi_slopes():
    s = (2.0 ** (-8.0 * (np.arange(N_MIX_HEADS) + 1) / N_MIX_HEADS)).astype(np.float32)
    return s[0::2], s[1::2]


def _mixer(x2, batch, seq, w_in, pe_k, w1_k, w2_k, pe_v, w1_v, w2_v):
    hd = HEAD_DIM
    sizes = [NSA_HEADS * hd] + [NSA_KV_GROUPS * hd] * 6 + [NSA_HEADS * 3] + [MOBA_HEADS * hd] * 3
    offs = np.concatenate([[0], np.cumsum(sizes)])
    sec = {n: w_in[:, offs[i]:offs[i + 1]] for i, n in enumerate(
        ["nq", "kc", "vc", "ks", "vs", "kw", "vw", "gl", "mq", "mk", "mv"])}
    order = ["nq", "ks", "vs", "kw", "vw", "mq", "mk", "mv"]
    cols, at = {}, 0
    for n in order:
        cols[n] = at // LANES
        at += sec[n].shape[1]
    w_main = jnp.concatenate([sec[n] for n in order], axis=1).astype(MXU_DTYPE)
    scale = hd ** -0.5
    col_scale = jnp.concatenate([
        jnp.full((1, sec[n].shape[1]), scale if n in ("nq", "mq") else 1.0, F32) for n in order], axis=1)
    per_group = NSA_HPG * 3
    gl_cols = [jnp.pad(sec["gl"][:, g * per_group:(g + 1) * per_group], ((0, 0), (0, LANES - per_group)))
               for g in range(NSA_KV_GROUPS)]
    w_aux = jnp.concatenate([sec["kc"], sec["vc"]] + gl_cols, axis=1).astype(MXU_DTYPE)
    cols["gl_aux"] = (2 * NSA_KV_GROUPS * hd) // LANES

    xb = x2.astype(MXU_DTYPE)
    tm = 512 if x2.shape[0] % 512 == 0 else 256
    proj = _matmul(xb, w_main, col_scale, MXU_DTYPE, tm, 512, "in_proj")
    aux = _matmul(xb, w_aux, jnp.ones((1, w_aux.shape[1]), F32), F32, tm, 256, "in_proj_aux")

    ns = seq // NSA_CMP_STRIDE

    def strips(col0):
        raw = aux[:, col0:col0 + NSA_KV_GROUPS * hd].reshape(batch, seq, NSA_KV_GROUPS, hd)
        return raw.transpose(0, 2, 1, 3).reshape(batch * NSA_KV_GROUPS, ns, NSA_CMP_STRIDE * hd)

    kc = _compress(strips(0), pe_k, w1_k, w2_k)
    vc = _compress(strips(NSA_KV_GROUPS * hd), pe_v, w1_v, w2_v)

    def values_t(name, heads):
        c0 = cols[name] * LANES
        v = proj[:, c0:c0 + heads * hd].reshape(batch, seq, heads * hd)
        return v.transpose(0, 2, 1).reshape(batch * heads * hd, seq)

    slope_n, slope_m = _alibi_slopes()
    sn = np.repeat(slope_n.reshape(NSA_KV_GROUPS, NSA_HPG), Q_BLK, axis=1)
    sn_rows = jnp.asarray(np.broadcast_to(sn[:, :, None], sn.shape + (LANES,)).copy())
    sn_lanes = jnp.asarray(sn[:, None, :].copy())
    sm = jnp.asarray(np.broadcast_to(slope_m[:, None, None], (MOBA_HEADS, 1, LANES)).copy())

    o_nsa = _nsa(proj, aux, kc, vc, values_t("vs", NSA_KV_GROUPS), sn_rows, sn_lanes, batch, seq, cols)
    o_moba = _moba(proj, values_t("mv", MOBA_HEADS), sm, batch, seq, cols)
    return o_nsa, o_moba


def _memory_xattn_ln(h, mem2, batch, seq, wq, wkv, wo, g, b):
    mem_len = mem2.shape[0] // batch
    e2 = wkv.shape[1]
    kv = _matmul(mem2.astype(MXU_DTYPE), wkv.astype(MXU_DTYPE), jnp.ones((1, e2), F32), MXU_DTYPE,
                 mem_len, e2 // 2, "xattn_kv")
    return _xattn_ln(h, kv, wq.astype(MXU_DTYPE), wo.astype(MXU_DTYPE), g, b, batch, seq, mem_len, 256)


def _pack_bf16_pairs(table):
    n, d = table.shape
    bits = lax.bitcast_convert_type(table.astype(jnp.bfloat16), jnp.uint16).astype(jnp.uint32)
    bits = bits.reshape(n, 2, d // (2 * LANES), LANES)
    return bits[:, 0] | (bits[:, 1] << 16)


def _peer_ln(h, wq, sub_keys, exp_u, exp_v, g, b):
    t, d = h.shape
    gw_t, eidx_t = _peer_route(h, wq.astype(MXU_DTYPE), sub_keys.astype(MXU_DTYPE), 256)
    uv = jnp.concatenate([_pack_bf16_pairs(exp_u), _pack_bf16_pairs(exp_v)], axis=1)
    f = _peer_experts(eidx_t, gw_t, h.reshape(t, PEER_SLAB, LANES), uv).reshape(t, d)
    return _add_ln(h, f, g, b, 256)


def kernel(x, mem, w_in, cmp_pe_k, cmp_w1_k, cmp_w2_k, cmp_pe_v, cmp_w1_v, cmp_w2_v, w_out, ln1_g, ln1_b,
           xa_wq, xa_wkv, xa_wo, ln2_g, ln2_b, peer_wq, peer_subkeys, peer_u, peer_v, ln3_g, ln3_b):
    batch, seq, d = x.shape
    assert seq % MOBA_STEP == 0 and WIN_SPAN <= seq <= NSA_SEL_BLOCK * LANES
    assert d == PEER_SLAB * LANES and w_in.shape[0] == DEPTH and PEER_TOPK == 2 * SUBLANES
    row = lambda v: v.reshape(1, d)
    h = x.reshape(batch * seq, d)
    mem2 = mem.reshape(-1, d)
    for l in range(DEPTH):
        o_nsa, o_moba = _mixer(h, batch, seq, w_in[l], cmp_pe_k[l], cmp_w1_k[l], cmp_w2_k[l],
                               cmp_pe_v[l], cmp_w1_v[l], cmp_w2_v[l])
        wo = w_out[l].astype(MXU_DTYPE)
        k1 = o_nsa.shape[1]
        h = _outproj_ln(o_nsa, o_moba, h, wo[:k1], wo[k1:], row(ln1_g[l]), row(ln1_b[l]), 256)
        h = _memory_xattn_ln(h, mem2, batch, seq, xa_wq[l], xa_wkv[l], xa_wo[l], row(ln2_g[l]), row(ln2_b[l]))
        h = _peer_ln(h, peer_wq[l], peer_subkeys[l], peer_u[l], peer_v[l], row(ln3_g[l]), row(ln3_b[l]))
    return h.reshape(batch, seq, d)
```

```python
import functools

import numpy as np
import jax
import jax.numpy as jnp
from jax import lax
from jax.experimental import pallas as pl
from jax.experimental.pallas import tpu as pltpu

F32 = jnp.float32
MXU_DTYPE = jnp.bfloat16
NEG_INF = float("-inf")

LANES = 128
SUBLANES = 8
VMEM_LIMIT = 48 * 1024 * 1024

HEAD_DIM = 128
N_MIX_HEADS = 16
NSA_HEADS = 8
NSA_KV_GROUPS = 2
NSA_HPG = NSA_HEADS // NSA_KV_GROUPS
NSA_CMP_STRIDE = 16
NSA_CMP_LEN = 32
NSA_SEL_BLOCK = 64
NSA_TOPN = 16
NSA_WINDOW = 512
NSA_FORCE_BONUS = 1.0e4
MOBA_HEADS = 8
MOBA_BLOCK = 256
MOBA_TOPK = 3
Q_BLK = 64
XA_HEADS = 4
XA_DIM = 128
PEER_HEADS = 8
PEER_NKEYS = 128
PEER_QDIM = 256
PEER_TOPK = 16
DEPTH = 1
DN_ALPHA = (2 * DEPTH) ** 0.25
LN_EPS = 1e-5

NSA_ROWS = NSA_HPG * Q_BLK
WIN_SPAN = NSA_WINDOW + 2 * Q_BLK
SEL_STEP_BLOCKS = SUBLANES
SEL_STEP = SEL_STEP_BLOCKS * NSA_SEL_BLOCK
MOBA_STEP_BLOCKS = 4
MOBA_STEP = MOBA_STEP_BLOCKS * MOBA_BLOCK
PEER_PICKS = PEER_HEADS * PEER_TOPK
PEER_SLAB = 16
PEER_SLAB_PITCH = 24
PEER_SLOTS = 4
PEER_TOK_TILE = LANES


def _cparams(*sem):
    return pltpu.CompilerParams(dimension_semantics=sem, vmem_limit_bytes=VMEM_LIMIT)


def _dot(a, b):
    return jnp.dot(a, b, preferred_element_type=F32)


def _dot_nt(a, b):
    return lax.dot_general(a, b, (((1,), (1,)), ((), ())), preferred_element_type=F32)


def _split3(x):
    hi = x.astype(MXU_DTYPE)
    r1 = x - hi.astype(F32)
    mid = r1.astype(MXU_DTYPE)
    lo = (r1 - mid.astype(F32)).astype(MXU_DTYPE)
    return hi, mid, lo


def _masked_softmax(s, mask):
    s = jnp.where(mask, s, NEG_INF)
    m = jnp.max(s, -1, keepdims=True)
    m = jnp.where(jnp.isfinite(m), m, 0.0)
    e = jnp.where(mask, jnp.exp(s - m), 0.0)
    d = jnp.sum(e, -1, keepdims=True)
    return e / jnp.where(d > 0, d, 1.0)


def _flash_partial(s, row_shift, vt):
    m = jnp.max(s, 0, keepdims=True)
    p = jnp.exp(s - jnp.where(m == NEG_INF, 0.0, m))
    return m - row_shift, jnp.sum(p, 0, keepdims=True), _dot(vt, p.astype(MXU_DTYPE))


def _online_step_t(carry, pieces, row_shift):
    m_i, l_i, acc = carry
    parts = [_flash_partial(s, row_shift, vt) for s, vt in pieces]
    m_new = m_i
    for m_p, _, _ in parts:
        m_new = jnp.maximum(m_new, m_p)
    m_safe = jnp.where(m_new == NEG_INF, 0.0, m_new)
    alpha = jnp.exp(m_i - m_safe)
    l_new, acc_new = alpha * l_i, alpha * acc
    for m_p, l_p, acc_p in parts:
        alpha = jnp.exp(m_p - m_safe)
        l_new, acc_new = l_new + alpha * l_p, acc_new + alpha * acc_p
    return m_new, l_new, acc_new


def _flash_init(n_q):
    return (jnp.full((1, n_q), NEG_INF, F32), jnp.zeros((1, n_q), F32), jnp.zeros((HEAD_DIM, n_q), F32))


def _layernorm(y, g, b):
    mu = jnp.mean(y, -1, keepdims=True)
    var = jnp.mean(jnp.square(y - mu), -1, keepdims=True)
    return (y - mu) * lax.rsqrt(var + LN_EPS) * g + b


def _top_select_t(score, row, n_pick):
    sentinel = float(score.shape[0])

    def body(_, carry):
        s, sel = carry
        m = jnp.max(s, 0, keepdims=True)
        idx = jnp.min(jnp.where(s == m, row, sentinel), 0, keepdims=True)
        hit = row == idx
        return jnp.where(hit, NEG_INF, s), jnp.where(hit, 1.0, sel)

    _, sel = lax.fori_loop(0, n_pick, body, (score, jnp.zeros_like(score)))
    return sel


def _top_values_t(score, row, n_pick):
    sentinel = float(score.shape[0])
    n = score.shape[1]
    out_row = lax.broadcasted_iota(jnp.int32, (n_pick, n), 0)

    def body(k, carry):
        s, vals, idxs = carry
        m = jnp.max(s, 0, keepdims=True)
        idx = jnp.min(jnp.where(s == m, row, sentinel), 0, keepdims=True)
        here = out_row == k
        return jnp.where(row == idx, NEG_INF, s), jnp.where(here, m, vals), jnp.where(here, idx, idxs)

    zeros = jnp.zeros((n_pick, n), F32)
    _, vals, idxs = lax.fori_loop(0, n_pick, body, (score, zeros, zeros))
    return vals, idxs


def _mm_kernel(a_ref, b_ref, s_ref, o_ref):
    acc = _dot(a_ref[...], b_ref[...])
    o_ref[...] = (acc * s_ref[...]).astype(o_ref.dtype)


def _matmul(a, b, col_scale, out_dtype, tm, tn, name):
    m, k = a.shape
    n = b.shape[1]
    return pl.pallas_call(
        _mm_kernel,
        out_shape=jax.ShapeDtypeStruct((m, n), out_dtype),
        grid=(m // tm, n // tn),
        in_specs=[pl.BlockSpec((tm, k), lambda i, j: (i, 0)),
                  pl.BlockSpec((k, tn), lambda i, j: (0, j)),
                  pl.BlockSpec((1, tn), lambda i, j: (0, j))],
        out_specs=pl.BlockSpec((tm, tn), lambda i, j: (i, j)),
        compiler_params=_cparams("parallel", "parallel"),
        name=name,
    )(a, b, col_scale)


def _proj_t_kernel(w_ref, x_ref, o_ref):
    o_ref[...] = _dot_nt(w_ref[...], x_ref[...]).astype(o_ref.dtype)


def _project_t(w_t, xb, batch, seq, tm):
    n, k = w_t.shape
    per = seq // tm
    return pl.pallas_call(
        _proj_t_kernel,
        out_shape=jax.ShapeDtypeStruct((batch * n, seq), MXU_DTYPE),
        grid=(batch * per,),
        in_specs=[pl.BlockSpec((n, k), lambda i: (0, 0)), pl.BlockSpec((tm, k), lambda i: (i, 0))],
        out_specs=pl.BlockSpec((n, tm), lambda i: (i // per, i % per)),
        compiler_params=_cparams("parallel"),
        name="in_proj_vt",
    )(w_t, xb)


def _compress_kernel(r_ref, pelo_ref, pehi_ref, w1lo_ref, w1hi_ref, w2_ref, o_ref):
    r = r_ref[0]
    ns = r.shape[0]
    lo = _dot((r + pelo_ref[...]).astype(MXU_DTYPE), w1lo_ref[...])
    hi = _dot((r + pehi_ref[...]).astype(MXU_DTYPE), w1hi_ref[...])
    hid = jax.nn.gelu(lo + pltpu.roll(hi, ns - 1, 0))
    o_ref[0] = _dot(hid.astype(MXU_DTYPE), w2_ref[...]).astype(o_ref.dtype)


def _compress(strips, pe, w1, w2):
    bg, ns, width = strips.shape
    half = NSA_CMP_LEN // 2
    pelo = pe[:half].reshape(1, width)
    pehi = pe[half:].reshape(1, width)
    w1lo = w1[:half].reshape(width, HEAD_DIM).astype(MXU_DTYPE)
    w1hi = w1[half:].reshape(width, HEAD_DIM).astype(MXU_DTYPE)
    const = lambda shape: pl.BlockSpec(shape, lambda i: (0,) * len(shape))
    return pl.pallas_call(
        _compress_kernel,
        out_shape=jax.ShapeDtypeStruct((bg, ns, HEAD_DIM), MXU_DTYPE),
        grid=(bg,),
        in_specs=[pl.BlockSpec((1, ns, width), lambda i: (i, 0, 0)),
                  const((1, width)), const((1, width)),
                  const((width, HEAD_DIM)), const((width, HEAD_DIM)), const((HEAD_DIM, HEAD_DIM))],
        out_specs=pl.BlockSpec((1, ns, HEAD_DIM), lambda i: (i, 0, 0)),
        compiler_params=_cparams("parallel"),
        name="nsa_compress",
    )(strips, pelo, pehi, w1lo, w1hi, w2.astype(MXU_DTYPE))


def _nsa_kernel(q_ref, kc_ref, vc_ref, ks_ref, vst_ref, kw_ref, vw_ref, gl_ref, slope_ref, slope_t_ref,
                o_ref, selt_ref, *, top_n):
    c = pl.program_id(2)
    q0 = c * Q_BLK
    rows = NSA_ROWS
    q = q_ref[...]
    qh = jnp.concatenate([q[:, h * HEAD_DIM:(h + 1) * HEAD_DIM] for h in range(NSA_HPG)], axis=0)
    slope = slope_ref[0][:, 0:1]
    slope_t = slope_t_ref[0]

    def tpos(width):
        return q0 + (lax.broadcasted_iota(jnp.int32, (rows, width), 0) & (Q_BLK - 1))

    kc = kc_ref[0]
    ns = kc.shape[0]
    cend = lax.broadcasted_iota(jnp.int32, (rows, ns), 1) * NSA_CMP_STRIDE + (NSA_CMP_LEN - 1)
    t_c = tpos(ns)
    s_c = _dot_nt(qh, kc) - slope * (t_c - cend).astype(F32)
    p_c = _masked_softmax(s_c, cend <= t_c)
    o_c = _dot(p_c.astype(MXU_DTYPE), vc_ref[0])

    p_sum = p_c[0:Q_BLK]
    for h in range(1, NSA_HPG):
        p_sum = p_sum + p_c[h * Q_BLK:(h + 1) * Q_BLK]
    p_two = jnp.concatenate([p_sum, p_sum], axis=0)
    ratio = NSA_SEL_BLOCK // NSA_CMP_STRIDE
    gj = lax.broadcasted_iota(jnp.int32, (LANES, ns), 0) * ratio
    gi = lax.broadcasted_iota(jnp.int32, (LANES, ns), 1)
    gather01 = jnp.where((gi >= gj - 1) & (gi <= gj + ratio - 1), 1.0, 0.0).astype(MXU_DTYPE)
    hi, mid, lo = _split3(p_two)
    imp_t = _dot_nt(gather01, hi) + _dot_nt(gather01, mid) + _dot_nt(gather01, lo)
    jrow_i = lax.broadcasted_iota(jnp.int32, (LANES, LANES), 0)
    forced = (jrow_i == 0) | (jrow_i == c) | (jrow_i == c - 1)
    valid = jrow_i <= c
    score_t = jnp.where(valid, imp_t + jnp.where(forced, NSA_FORCE_BONUS, 0.0), NEG_INF)
    sel_t = jnp.where(valid, _top_select_t(score_t, jrow_i.astype(F32), top_n), 0.0)
    selt_ref[...] = jnp.concatenate([sel_t, sel_t], axis=1)

    off = (lax.broadcasted_iota(jnp.int32, (SEL_STEP, rows), 0)
           - (lax.broadcasted_iota(jnp.int32, (SEL_STEP, rows), 1) & (Q_BLK - 1)))
    bias_t = slope_t * (-off).astype(F32)

    def step_picks(kb):
        return selt_ref[pl.ds(pl.multiple_of(kb * SEL_STEP_BLOCKS, SEL_STEP_BLOCKS), SEL_STEP_BLOCKS), :]

    def sel_step(kb, carry, causal):
        k0 = pl.multiple_of(kb * SEL_STEP, SEL_STEP)
        picks = step_picks(kb)
        n_half = SEL_STEP // 2
        halves = []
        for hf in range(2):
            kh = pl.multiple_of(k0 + hf * n_half, n_half)
            s = _dot_nt(ks_ref[pl.ds(kh, n_half), :], qh) - bias_t[hf * n_half:(hf + 1) * n_half]
            masked = []
            for i in range(SEL_STEP_BLOCKS // 2):
                r = slice(i * NSA_SEL_BLOCK, (i + 1) * NSA_SEL_BLOCK)
                b = hf * (SEL_STEP_BLOCKS // 2) + i
                ok = picks[b:b + 1, :] > 0.5
                if causal:
                    ok = ok & (off[b * NSA_SEL_BLOCK:(b + 1) * NSA_SEL_BLOCK] <= q0 - k0)
                masked.append(jnp.where(ok, s[r], NEG_INF))
            halves.append((jnp.concatenate(masked, axis=0), vst_ref[:, pl.ds(kh, n_half)]))
        return _online_step_t(carry, halves, slope_t * (q0 - k0).astype(F32))

    def maybe_step(kb, carry):
        any_pick = jnp.max(jnp.max(step_picks(kb), 1, keepdims=True), 0, keepdims=True)[0, 0] > 0.5
        return lax.cond(any_pick, lambda cr: sel_step(kb, cr, causal=False), lambda cr: cr, carry)

    last = c // SEL_STEP_BLOCKS
    carry = lax.fori_loop(0, last, maybe_step, _flash_init(rows))
    _, l_s, acc_s = sel_step(last, carry, causal=True)
    o_s = (acc_s / jnp.where(l_s > 0, l_s, 1.0)).T

    w0 = pl.multiple_of(jnp.maximum(q0 - (WIN_SPAN - Q_BLK), 0), Q_BLK)
    t_w = tpos(WIN_SPAN)
    dist_w = t_w - (w0 + lax.broadcasted_iota(jnp.int32, (rows, WIN_SPAN), 1))
    s_w = _dot_nt(qh, kw_ref[pl.ds(w0, WIN_SPAN), :]) - slope * dist_w.astype(F32)
    p_w = _masked_softmax(s_w, (dist_w >= 0) & (dist_w < NSA_WINDOW))
    o_w = _dot(p_w.astype(MXU_DTYPE), vw_ref[pl.ds(w0, WIN_SPAN), :])

    gates = 1.0 / (1.0 + jnp.exp(-gl_ref[...]))
    for h in range(NSA_HPG):
        r = slice(h * Q_BLK, (h + 1) * Q_BLK)
        o = (gates[:, 3 * h:3 * h + 1] * o_c[r] + gates[:, 3 * h + 1:3 * h + 2] * o_s[r]
             + gates[:, 3 * h + 2:3 * h + 3] * o_w[r])
        o_ref[:, h * HEAD_DIM:(h + 1) * HEAD_DIM] = o.astype(o_ref.dtype)


def _nsa(proj, aux, kc, vc, vs_t, slopes, slopes_t, batch, seq, cols):
    n_chunks = seq // Q_BLK
    ns = kc.shape[1]
    gw = NSA_HPG * HEAD_DIM
    top_n = min(NSA_TOPN, seq // NSA_SEL_BLOCK)
    kv = lambda base: pl.BlockSpec((seq, HEAD_DIM), lambda b, g, c: (b, base + g))
    cmp_spec = pl.BlockSpec((1, ns, HEAD_DIM), lambda b, g, c: (b * NSA_KV_GROUPS + g, 0, 0))
    return pl.pallas_call(
        functools.partial(_nsa_kernel, top_n=top_n),
        out_shape=jax.ShapeDtypeStruct((batch * seq, NSA_HEADS * HEAD_DIM), MXU_DTYPE),
        grid=(batch, NSA_KV_GROUPS, n_chunks),
        in_specs=[pl.BlockSpec((Q_BLK, gw), lambda b, g, c: (b * n_chunks + c, g)),
                  cmp_spec, cmp_spec,
                  kv(cols["ks"]),
                  pl.BlockSpec((HEAD_DIM, seq), lambda b, g, c: (b * cols["vt_heads"] + cols["vs_t"] + g, 0)),
                  kv(cols["kw"]), kv(cols["vw"]),
                  pl.BlockSpec((Q_BLK, LANES), lambda b, g, c: (b * n_chunks + c, cols["gl_aux"] + g)),
                  pl.BlockSpec((1, NSA_ROWS, LANES), lambda b, g, c: (g, 0, 0)),
                  pl.BlockSpec((1, 1, NSA_ROWS), lambda b, g, c: (g, 0, 0))],
        out_specs=pl.BlockSpec((Q_BLK, gw), lambda b, g, c: (b * n_chunks + c, g)),
        scratch_shapes=[pltpu.VMEM((LANES, NSA_ROWS), F32)],
        compiler_params=_cparams("parallel", "parallel", "arbitrary"),
        name="nsa_attention",
    )(proj, kc, vc, proj, vs_t, proj, proj, aux, slopes, slopes_t)


def _moba_kernel(q_ref, k_ref, vt_ref, slope_ref, o_ref, kmean_ref, selt_ref, *, top_m):
    cb = pl.program_id(2)
    seq = k_ref.shape[0]
    n_blocks = seq // MOBA_BLOCK
    nq = MOBA_BLOCK

    @pl.when(cb == 0)
    def _():
        kmean_ref[...] = jnp.zeros_like(kmean_ref)
        kf = k_ref[...].astype(F32).reshape(n_blocks, MOBA_BLOCK, HEAD_DIM)
        kmean_ref[0:n_blocks, :] = jnp.sum(kf, axis=1) * (1.0 / MOBA_BLOCK)

    q = q_ref[...]
    slope = slope_ref[0][:, 0:1]
    brow_i = lax.broadcasted_iota(jnp.int32, (LANES, nq), 0)
    gate_t = jnp.where(brow_i < cb, _dot_nt(kmean_ref[...].astype(MXU_DTYPE), q), NEG_INF)
    selt_ref[...] = jnp.where(brow_i < cb, _top_select_t(gate_t, brow_i.astype(F32), top_m), 0.0)

    off = (lax.broadcasted_iota(jnp.int32, (MOBA_STEP, nq), 0)
           - lax.broadcasted_iota(jnp.int32, (MOBA_STEP, nq), 1))
    bias_t = slope * (-off).astype(F32)
    q0 = cb * MOBA_BLOCK

    def past_step(st, carry):
        k0 = pl.multiple_of(st * MOBA_STEP, MOBA_STEP)
        pair = selt_ref[pl.ds(pl.multiple_of((st // 2) * SUBLANES, SUBLANES), SUBLANES), :]
        picks = jnp.where(st % 2 == 0, pair[0:MOBA_STEP_BLOCKS], pair[MOBA_STEP_BLOCKS:2 * MOBA_STEP_BLOCKS])
        n_half = MOBA_STEP // 2
        halves = []
        for hf in range(2):
            kh = pl.multiple_of(k0 + hf * n_half, n_half)
            s = _dot_nt(k_ref[pl.ds(kh, n_half), :], q) - bias_t[hf * n_half:(hf + 1) * n_half]
            masked = []
            for i in range(MOBA_STEP_BLOCKS // 2):
                b = hf * (MOBA_STEP_BLOCKS // 2) + i
                masked.append(jnp.where(picks[b:b + 1, :] > 0.5, s[i * MOBA_BLOCK:(i + 1) * MOBA_BLOCK], NEG_INF))
            halves.append((jnp.concatenate(masked, axis=0), vt_ref[:, pl.ds(kh, n_half)]))
        return _online_step_t(carry, halves, slope * (q0 - k0).astype(F32))

    n_steps = (cb + MOBA_STEP_BLOCKS - 1) // MOBA_STEP_BLOCKS
    carry = lax.fori_loop(0, n_steps, past_step, _flash_init(nq))
    kc0 = pl.multiple_of(q0, MOBA_BLOCK)
    s = _dot_nt(k_ref[pl.ds(kc0, MOBA_BLOCK), :], q) - bias_t[0:MOBA_BLOCK]
    s = jnp.where(off[0:MOBA_BLOCK] <= 0, s, NEG_INF)
    _, l, acc = _online_step_t(carry, [(s, vt_ref[:, pl.ds(kc0, MOBA_BLOCK)])], jnp.zeros((1, 1), F32))
    o_ref[...] = (acc / jnp.where(l > 0, l, 1.0)).T.astype(o_ref.dtype)


def _moba(proj, mv_t, slopes, batch, seq, cols):
    n_blocks = seq // MOBA_BLOCK
    top_m = min(MOBA_TOPK, n_blocks)
    return pl.pallas_call(
        functools.partial(_moba_kernel, top_m=top_m),
        out_shape=jax.ShapeDtypeStruct((batch * seq, MOBA_HEADS * HEAD_DIM), MXU_DTYPE),
        grid=(batch, MOBA_HEADS, n_blocks),
        in_specs=[pl.BlockSpec((MOBA_BLOCK, HEAD_DIM), lambda b, h, c: (b * n_blocks + c, cols["mq"] + h)),
                  pl.BlockSpec((seq, HEAD_DIM), lambda b, h, c: (b, cols["mk"] + h)),
                  pl.BlockSpec((HEAD_DIM, seq), lambda b, h, c: (b * cols["vt_heads"] + cols["mv_t"] + h, 0)),
                  pl.BlockSpec((1, 1, LANES), lambda b, h, c: (h, 0, 0))],
        out_specs=pl.BlockSpec((MOBA_BLOCK, HEAD_DIM), lambda b, h, c: (b * n_blocks + c, h)),
        scratch_shapes=[pltpu.VMEM((LANES, HEAD_DIM), F32), pltpu.VMEM((LANES, MOBA_BLOCK), F32)],
        compiler_params=_cparams("parallel", "parallel", "arbitrary"),
        name="moba_attention",
    )(proj, proj, mv_t, slopes)


def _outproj_ln_kernel(a1_ref, a2_ref, x_ref, w1_ref, w2_ref, g_ref, b_ref, o_ref):
    y = DN_ALPHA * x_ref[...] + _dot(a1_ref[...], w1_ref[...]) + _dot(a2_ref[...], w2_ref[...])
    o_ref[...] = _layernorm(y, g_ref[...], b_ref[...])


def _outproj_ln(a1, a2, x, w1, w2, g, b, tm):
    t, d = x.shape
    k1, k2 = a1.shape[1], a2.shape[1]
    const = lambda shape: pl.BlockSpec(shape, lambda i: (0, 0))
    return pl.pallas_call(
        _outproj_ln_kernel,
        out_shape=jax.ShapeDtypeStruct((t, d), F32),
        grid=(t // tm,),
        in_specs=[pl.BlockSpec((tm, k1), lambda i: (i, 0)), pl.BlockSpec((tm, k2), lambda i: (i, 0)),
                  pl.BlockSpec((tm, d), lambda i: (i, 0)),
                  const((k1, d)), const((k2, d)), const((1, d)), const((1, d))],
        out_specs=pl.BlockSpec((tm, d), lambda i: (i, 0)),
        compiler_params=_cparams("parallel"),
        name="out_proj_ln",
    )(a1, a2, x, w1, w2, g, b)


def _xattn_ln_kernel(h_ref, wq_ref, k_ref, v_ref, wo_ref, g_ref, b_ref, o_ref, o_lowp_ref):
    h = h_ref[...]
    q = (_dot(h.astype(MXU_DTYPE), wq_ref[...]) * (XA_DIM ** -0.5)).astype(MXU_DTYPE)
    outs = []
    for hd in range(XA_HEADS):
        cs = slice(hd * XA_DIM, (hd + 1) * XA_DIM)
        s = _dot_nt(q[:, cs], k_ref[:, cs])
        m = jnp.max(s, -1, keepdims=True)
        e = jnp.exp(s - m)
        p = e / jnp.sum(e, -1, keepdims=True)
        outs.append(_dot(p.astype(MXU_DTYPE), v_ref[:, cs]).astype(MXU_DTYPE))
    o = jnp.concatenate(outs, axis=1)
    y = _layernorm(DN_ALPHA * h + _dot(o, wo_ref[...]), g_ref[...], b_ref[...])
    o_ref[...] = y
    o_lowp_ref[...] = y.astype(o_lowp_ref.dtype)


def _xattn_ln(h, kv, wq, wo, g, b, batch, seq, mem_len, tm):
    t, d = h.shape
    e = XA_HEADS * XA_DIM
    nt = seq // tm
    const = lambda shape: pl.BlockSpec(shape, lambda bi, i: (0, 0))
    return pl.pallas_call(
        _xattn_ln_kernel,
        out_shape=(jax.ShapeDtypeStruct((t, d), F32), jax.ShapeDtypeStruct((t, d), MXU_DTYPE)),
        grid=(batch, nt),
        in_specs=[pl.BlockSpec((tm, d), lambda bi, i: (bi * nt + i, 0)),
                  const((d, e)),
                  pl.BlockSpec((mem_len, e), lambda bi, i: (bi, 0)),
                  pl.BlockSpec((mem_len, e), lambda bi, i: (bi, 1)),
                  const((e, d)), const((1, d)), const((1, d))],
        out_specs=(pl.BlockSpec((tm, d), lambda bi, i: (bi * nt + i, 0)),
                   pl.BlockSpec((tm, d), lambda bi, i: (bi * nt + i, 0))),
        compiler_params=_cparams("parallel", "parallel"),
        name="xattn_ln",
    )(h, wq, kv, kv, wo, g, b)


def _peer_route_kernel(h_ref, wq_ref, sk_ref, gw_ref, ei_ref):
    hh = pl.program_id(1)
    tm = h_ref.shape[0]
    k_top = PEER_TOPK
    q = _dot(h_ref[...], wq_ref[...]).astype(MXU_DTYPE)
    key_row = lax.broadcasted_iota(jnp.int32, (PEER_NKEYS, tm), 0).astype(F32)
    tops = []
    for half in range(2):
        s = _dot_nt(sk_ref[0, half], q[:, half * PEER_NKEYS:(half + 1) * PEER_NKEYS])
        tops.append(_top_values_t(s, key_row, k_top))
    (v0, i0), (v1, i1) = tops

    def grid(a, b):
        first = [a[0:1] + b]
        mid = [a[i:i + 1] + b[0:SUBLANES] for i in range(1, SUBLANES)]
        return jnp.concatenate(first + mid + [a[SUBLANES:] + b[0:1]], axis=0)

    cand = grid(v0, v1)
    cidx = grid(i0 * float(PEER_NKEYS), i1)
    n_cand = cand.shape[0]
    cand_row = lax.broadcasted_iota(jnp.int32, (n_cand, tm), 0).astype(F32)
    pick_row = lax.broadcasted_iota(jnp.int32, (k_top, tm), 0)

    def pick_expert(k, carry):
        cd, tv, te = carry
        m = jnp.max(cd, 0, keepdims=True)
        j = jnp.min(jnp.where(cd == m, cand_row, float(n_cand)), 0, keepdims=True)
        hit = cand_row == j
        e = jnp.sum(jnp.where(hit, cidx, 0.0), 0, keepdims=True)
        here = pick_row == k
        return jnp.where(hit, NEG_INF, cd), jnp.where(here, m, tv), jnp.where(here, e, te)

    zeros = jnp.zeros((k_top, tm), F32)
    _, tv, te = lax.fori_loop(0, k_top, pick_expert, (cand, zeros, zeros))
    ex = jnp.exp(tv - jnp.max(tv, 0, keepdims=True))
    rows = pl.ds(pl.multiple_of(hh * k_top, k_top), k_top)
    gw_ref[rows, :] = ex / jnp.sum(ex, 0, keepdims=True)
    ei_ref[rows, :] = te.astype(jnp.int32)


def _peer_route(h, wq, sub_keys, tm):
    t, d = h.shape
    return pl.pallas_call(
        _peer_route_kernel,
        out_shape=(jax.ShapeDtypeStruct((PEER_PICKS, t), F32), jax.ShapeDtypeStruct((PEER_PICKS, t), jnp.int32)),
        grid=(t // tm, PEER_HEADS),
        in_specs=[pl.BlockSpec((tm, d), lambda i, hh: (i, 0)),
                  pl.BlockSpec((d, PEER_QDIM), lambda i, hh: (0, hh)),
                  pl.BlockSpec((1, 2, PEER_NKEYS, PEER_QDIM // 2), lambda i, hh: (hh, 0, 0, 0))],
        out_specs=(pl.BlockSpec((PEER_PICKS, tm), lambda i, hh: (0, i)),
                   pl.BlockSpec((PEER_PICKS, tm), lambda i, hh: (0, i))),
        compiler_params=_cparams("parallel", "arbitrary"),
        name="peer_route",
    )(h, wq, sub_keys)


def _peer_expert_kernel(ei_ref, gw_ref, x_ref, uv_ref, o_ref, *scratch):
    bufs, sem_ref = scratch[:PEER_SLOTS], scratch[PEER_SLOTS]
    n_tok = x_ref.shape[0]
    half = PEER_SLAB // 2
    groups = PEER_PICKS // SUBLANES
    lookahead = PEER_SLOTS - 1

    def slab(slot, j):
        return bufs[slot].at[pl.ds(j * PEER_SLAB_PITCH, PEER_SLAB)]

    def issue(tok, slot):
        for j in range(PEER_PICKS):
            pltpu.make_async_copy(uv_ref.at[ei_ref[j, tok]], slab(slot, j), sem_ref.at[slot]).start()

    def wait(slot):
        for j in range(PEER_PICKS):
            pltpu.make_async_copy(uv_ref.at[0], slab(slot, j), sem_ref.at[slot]).wait()

    gw_t = gw_ref[...]
    tok_lane = lax.broadcasted_iota(jnp.int32, gw_t.shape, 1)

    def compute(tok, slot):
        def words(j0, r):
            w = bufs[slot][pl.ds(j0 * PEER_SLAB_PITCH + r, SUBLANES, stride=PEER_SLAB_PITCH), :]
            return (lax.bitcast_convert_type(w << 16, F32),
                    lax.bitcast_convert_type(w & jnp.uint32(0xFFFF0000), F32))

        xs = [x_ref[tok, pl.ds(r, 1), :] for r in range(2 * half)]
        parts = []
        for jg in range(groups):
            acc = None
            for r in range(half):
                lo, hi = words(jg * SUBLANES, r)
                term = lo * xs[r] + hi * xs[half + r]
                acc = term if acc is None else acc + term
            parts.append(jnp.sum(acc, -1, keepdims=True))
        a = jnp.concatenate(parts, axis=0)
        gw_col = jnp.sum(jnp.where(tok_lane == tok, gw_t, 0.0), -1, keepdims=True)
        w = gw_col * jax.nn.gelu(a)
        ws = [w[jg * SUBLANES:(jg + 1) * SUBLANES] for jg in range(groups)]
        for r in range(half):
            acc_lo = acc_hi = None
            for jg in range(groups):
                lo, hi = words(jg * SUBLANES, half + r)
                acc_lo = lo * ws[jg] if acc_lo is None else acc_lo + lo * ws[jg]
                acc_hi = hi * ws[jg] if acc_hi is None else acc_hi + hi * ws[jg]
            o_ref[tok, pl.ds(r, 1), :] = jnp.sum(acc_lo, 0, keepdims=True)
            o_ref[tok, pl.ds(half + r, 1), :] = jnp.sum(acc_hi, 0, keepdims=True)

    def round_of_slots(base, n_issue):
        for s in range(PEER_SLOTS):
            wait(s)
            if s < n_issue:
                issue(base + s + lookahead, (s + lookahead) % PEER_SLOTS)
            compute(base + s, s)

    for s in range(lookahead):
        issue(s, s)
    n_rounds = n_tok // PEER_SLOTS

    def round_body(i, _):
        round_of_slots(i * PEER_SLOTS, PEER_SLOTS)
        return 0

    lax.fori_loop(0, n_rounds - 1, round_body, 0)
    round_of_slots((n_rounds - 1) * PEER_SLOTS, PEER_SLOTS - lookahead)


def _peer_experts(eidx_t, gw_t, x3, uv):
    t = x3.shape[0]
    tt = PEER_TOK_TILE
    return pl.pallas_call(
        _peer_expert_kernel,
        out_shape=jax.ShapeDtypeStruct((t, PEER_SLAB, LANES), F32),
        grid=(t // tt,),
        in_specs=[pl.BlockSpec((PEER_PICKS, tt), lambda i: (0, i), memory_space=pltpu.SMEM),
                  pl.BlockSpec((PEER_PICKS, tt), lambda i: (0, i)),
                  pl.BlockSpec((tt, PEER_SLAB, LANES), lambda i: (i, 0, 0)),
                  pl.BlockSpec(memory_space=pl.ANY)],
        out_specs=pl.BlockSpec((tt, PEER_SLAB, LANES), lambda i: (i, 0, 0)),
        scratch_shapes=[pltpu.VMEM((PEER_PICKS * PEER_SLAB_PITCH, LANES), jnp.uint32) for _ in range(PEER_SLOTS)]
                       + [pltpu.SemaphoreType.DMA((PEER_SLOTS,))],
        compiler_params=_cparams("arbitrary"),
        name="peer_experts",
    )(eidx_t, gw_t, x3, uv)


def _add_ln_kernel(h_ref, f_ref, g_ref, b_ref, o_ref):
    o_ref[...] = _layernorm(DN_ALPHA * h_ref[...] + f_ref[...], g_ref[...], b_ref[...])


def _add_ln(h, f, g, b, tm):
    t, d = h.shape
    row = pl.BlockSpec((tm, d), lambda i: (i, 0))
    const = pl.BlockSpec((1, d), lambda i: (0, 0))
    return pl.pallas_call(
        _add_ln_kernel,
        out_shape=jax.ShapeDtypeStruct((t, d), F32),
        grid=(t // tm,),
        in_specs=[row, row, const, const],
        out_specs=row,
        compiler_params=_cparams("parallel"),
        name="add_ln",
    )(h, f, g, b)


def _alibi_slopes():
    s = (2.0 ** (-8.0 * (np.arange(N_MIX_HEADS) + 1) / N_MIX_HEADS)).astype(np.float32)
    return s[0::2], s[1::2]


def _mixer(x2, batch, seq, w_in, pe_k, w1_k, w2_k, pe_v, w1_v, w2_v):
    hd = HEAD_DIM
    sizes = [NSA_HEADS * hd] + [NSA_KV_GROUPS * hd] * 6 + [NSA_HEADS * 3] + [MOBA_HEADS * hd] * 3
    offs = np.concatenate([[0], np.cumsum(sizes)])
    sec = {n: w_in[:, offs[i]:offs[i + 1]] for i, n in enumerate(
        ["nq", "kc", "vc", "ks", "vs", "kw", "vw", "gl", "mq", "mk", "mv"])}
    order = ["nq", "ks", "kw", "vw", "mq", "mk"]
    cols, at = {}, 0
    for n in order:
        cols[n] = at // LANES
        at += sec[n].shape[1]
    w_main = jnp.concatenate([sec[n] for n in order], axis=1).astype(MXU_DTYPE)
    scale = hd ** -0.5
    col_scale = jnp.concatenate([
        jnp.full((1, sec[n].shape[1]), scale if n in ("nq", "mq") else 1.0, F32) for n in order], axis=1)
    per_group = NSA_HPG * 3
    gl_cols = [jnp.pad(sec["gl"][:, g * per_group:(g + 1) * per_group], ((0, 0), (0, LANES - per_group)))
               for g in range(NSA_KV_GROUPS)]
    w_aux = jnp.concatenate([sec["kc"], sec["vc"]] + gl_cols, axis=1).astype(MXU_DTYPE)
    cols["gl_aux"] = (2 * NSA_KV_GROUPS * hd) // LANES

    xb = x2.astype(MXU_DTYPE)
    tm = 512 if x2.shape[0] % 512 == 0 else 256
    proj = _matmul(xb, w_main, col_scale, MXU_DTYPE, tm, 768, "in_proj")
    w_vt = jnp.concatenate([sec["vs"], sec["mv"]], axis=1).T.astype(MXU_DTYPE)
    v_t = _project_t(w_vt, xb, batch, seq, tm)
    cols["vs_t"], cols["mv_t"], cols["vt_heads"] = 0, NSA_KV_GROUPS, NSA_KV_GROUPS + MOBA_HEADS
    aux = _matmul(xb, w_aux, jnp.ones((1, w_aux.shape[1]), F32), F32, tm, 256, "in_proj_aux")

    ns = seq // NSA_CMP_STRIDE

    def strips(col0):
        raw = aux[:, col0:col0 + NSA_KV_GROUPS * hd].reshape(batch, seq, NSA_KV_GROUPS, hd)
        return raw.transpose(0, 2, 1, 3).reshape(batch * NSA_KV_GROUPS, ns, NSA_CMP_STRIDE * hd)

    kc = _compress(strips(0), pe_k, w1_k, w2_k)
    vc = _compress(strips(NSA_KV_GROUPS * hd), pe_v, w1_v, w2_v)

    slope_n, slope_m = _alibi_slopes()
    sn = np.repeat(slope_n.reshape(NSA_KV_GROUPS, NSA_HPG), Q_BLK, axis=1)
    sn_rows = jnp.asarray(np.broadcast_to(sn[:, :, None], sn.shape + (LANES,)).copy())
    sn_lanes = jnp.asarray(sn[:, None, :].copy())
    sm = jnp.asarray(np.broadcast_to(slope_m[:, None, None], (MOBA_HEADS, 1, LANES)).copy())

    o_nsa = _nsa(proj, aux, kc, vc, v_t, sn_rows, sn_lanes, batch, seq, cols)
    o_moba = _moba(proj, v_t, sm, batch, seq, cols)
    return o_nsa, o_moba


def _memory_xattn_ln(h, mem2, batch, seq, wq, wkv, wo, g, b):
    mem_len = mem2.shape[0] // batch
    e2 = wkv.shape[1]
    kv = _matmul(mem2.astype(MXU_DTYPE), wkv.astype(MXU_DTYPE), jnp.ones((1, e2), F32), MXU_DTYPE,
                 mem_len, e2 // 2, "xattn_kv")
    return _xattn_ln(h, kv, wq.astype(MXU_DTYPE), wo.astype(MXU_DTYPE), g, b, batch, seq, mem_len, 256)


def _pack_tables_kernel(u_ref, v_ref, o_ref):
    def words(x):
        bits = lax.bitcast_convert_type(x.astype(jnp.bfloat16).astype(F32), jnp.uint32)
        half = x.shape[1] // 2
        return (bits[:, :half] >> 16) | (bits[:, half:] & jnp.uint32(0xFFFF0000))

    n = u_ref.shape[0]
    per_table = PEER_SLAB // 2
    for t, ref in enumerate((u_ref, v_ref)):
        w = words(ref[...])
        for r in range(per_table):
            o_ref[pl.ds(t * per_table + r, n, stride=PEER_SLAB), :] = w[:, r * LANES:(r + 1) * LANES]


def _pack_expert_tables(exp_u, exp_v):
    n_exp, d = exp_u.shape
    blk = 256
    packed = pl.pallas_call(
        _pack_tables_kernel,
        out_shape=jax.ShapeDtypeStruct((n_exp * PEER_SLAB, LANES), jnp.uint32),
        grid=(n_exp // blk,),
        in_specs=[pl.BlockSpec((blk, d), lambda i: (i, 0)), pl.BlockSpec((blk, d), lambda i: (i, 0))],
        out_specs=pl.BlockSpec((blk * PEER_SLAB, LANES), lambda i: (i, 0)),
        compiler_params=_cparams("parallel"),
        name="peer_pack_tables",
    )(exp_u, exp_v)
    return packed.reshape(n_exp, PEER_SLAB, LANES)


def _peer_ln(h, h_lowp, wq, sub_keys, exp_u, exp_v, g, b):
    t, d = h.shape
    gw_t, eidx_t = _peer_route(h_lowp, wq.astype(MXU_DTYPE), sub_keys.astype(MXU_DTYPE), 256)
    uv = _pack_expert_tables(exp_u, exp_v)
    f = _peer_experts(eidx_t, gw_t, h.reshape(t, PEER_SLAB, LANES), uv).reshape(t, d)
    return _add_ln(h, f, g, b, 256)


def kernel(x, mem, w_in, cmp_pe_k, cmp_w1_k, cmp_w2_k, cmp_pe_v, cmp_w1_v, cmp_w2_v, w_out, ln1_g, ln1_b,
           xa_wq, xa_wkv, xa_wo, ln2_g, ln2_b, peer_wq, peer_subkeys, peer_u, peer_v, ln3_g, ln3_b):
    batch, seq, d = x.shape
    assert seq % MOBA_STEP == 0 and WIN_SPAN <= seq <= NSA_SEL_BLOCK * LANES
    assert d == PEER_SLAB * LANES and w_in.shape[0] == DEPTH and PEER_TOPK == 2 * SUBLANES
    row = lambda v: v.reshape(1, d)
    h = x.reshape(batch * seq, d)
    mem2 = mem.reshape(-1, d)
    for l in range(DEPTH):
        o_nsa, o_moba = _mixer(h, batch, seq, w_in[l], cmp_pe_k[l], cmp_w1_k[l], cmp_w2_k[l],
                               cmp_pe_v[l], cmp_w1_v[l], cmp_w2_v[l])
        wo = w_out[l].astype(MXU_DTYPE)
        k1 = o_nsa.shape[1]
        h = _outproj_ln(o_nsa, o_moba, h, wo[:k1], wo[k1:], row(ln1_g[l]), row(ln1_b[l]), 256)
        h, h_lowp = _memory_xattn_ln(h, mem2, batch, seq, xa_wq[l], xa_wkv[l], xa_wo[l],
                                     row(ln2_g[l]), row(ln2_b[l]))
        h = _peer_ln(h, h_lowp, peer_wq[l], peer_subkeys[l], peer_u[l], peer_v[l], row(ln3_g[l]), row(ln3_b[l]))
    return h.reshape(batch, seq, d)
```

```python
import functools

import numpy as np
import jax
import jax.numpy as jnp
from jax import lax
from jax.experimental import pallas as pl
from jax.experimental.pallas import tpu as pltpu

F32 = jnp.float32
MXU_DTYPE = jnp.bfloat16
NEG_INF = float("-inf")

LANES = 128
SUBLANES = 8
VMEM_LIMIT = 48 * 1024 * 1024

HEAD_DIM = 128
N_MIX_HEADS = 16
NSA_HEADS = 8
NSA_KV_GROUPS = 2
NSA_HPG = NSA_HEADS // NSA_KV_GROUPS
NSA_CMP_STRIDE = 16
NSA_CMP_LEN = 32
NSA_SEL_BLOCK = 64
NSA_TOPN = 16
NSA_WINDOW = 512
NSA_FORCE_BONUS = 1.0e4
MOBA_HEADS = 8
MOBA_BLOCK = 256
MOBA_TOPK = 3
Q_BLK = 64
XA_HEADS = 4
XA_DIM = 128
PEER_HEADS = 8
PEER_NKEYS = 128
PEER_QDIM = 256
PEER_TOPK = 16
DEPTH = 1
DN_ALPHA = (2 * DEPTH) ** 0.25
LN_EPS = 1e-5

NSA_ROWS = NSA_HPG * Q_BLK
WIN_SPAN = NSA_WINDOW + 2 * Q_BLK
SEL_STEP_BLOCKS = 2 * SUBLANES
SEL_STEP = SEL_STEP_BLOCKS * NSA_SEL_BLOCK
MOBA_STEP_BLOCKS = 4
MOBA_STEP = MOBA_STEP_BLOCKS * MOBA_BLOCK
PEER_PICKS = PEER_HEADS * PEER_TOPK
PEER_SLAB = 16
PEER_SLAB_PITCH = 24
PEER_SLOTS = 4
PEER_TOK_TILE = LANES


def _cparams(*sem):
    return pltpu.CompilerParams(dimension_semantics=sem, vmem_limit_bytes=VMEM_LIMIT)


def _dot(a, b):
    return jnp.dot(a, b, preferred_element_type=F32)


def _dot_nt(a, b):
    return lax.dot_general(a, b, (((1,), (1,)), ((), ())), preferred_element_type=F32)


def _split3(x):
    hi = x.astype(MXU_DTYPE)
    r1 = x - hi.astype(F32)
    mid = r1.astype(MXU_DTYPE)
    lo = (r1 - mid.astype(F32)).astype(MXU_DTYPE)
    return hi, mid, lo


def _masked_softmax(s, mask):
    s = jnp.where(mask, s, NEG_INF)
    m = jnp.max(s, -1, keepdims=True)
    m = jnp.where(jnp.isfinite(m), m, 0.0)
    e = jnp.where(mask, jnp.exp(s - m), 0.0)
    d = jnp.sum(e, -1, keepdims=True)
    return e / jnp.where(d > 0, d, 1.0)


def _flash_partial(s, row_shift, vt):
    m = jnp.max(s, 0, keepdims=True)
    p = jnp.exp(s - jnp.where(m == NEG_INF, 0.0, m))
    return m - row_shift, jnp.sum(p, 0, keepdims=True), _dot(vt, p.astype(MXU_DTYPE))


def _online_step_t(carry, pieces, row_shift):
    m_i, l_i, acc = carry
    parts = [_flash_partial(s, row_shift, vt) for s, vt in pieces]
    m_new = m_i
    for m_p, _, _ in parts:
        m_new = jnp.maximum(m_new, m_p)
    m_safe = jnp.where(m_new == NEG_INF, 0.0, m_new)
    alpha = jnp.exp(m_i - m_safe)
    l_new, acc_new = alpha * l_i, alpha * acc
    for m_p, l_p, acc_p in parts:
        alpha = jnp.exp(m_p - m_safe)
        l_new, acc_new = l_new + alpha * l_p, acc_new + alpha * acc_p
    return m_new, l_new, acc_new


def _flash_init(n_q):
    return (jnp.full((1, n_q), NEG_INF, F32), jnp.zeros((1, n_q), F32), jnp.zeros((HEAD_DIM, n_q), F32))


def _layernorm(y, g, b):
    mu = jnp.mean(y, -1, keepdims=True)
    var = jnp.mean(jnp.square(y - mu), -1, keepdims=True)
    return (y - mu) * lax.rsqrt(var + LN_EPS) * g + b


def _top_select_t(score, row, n_pick):
    sentinel = float(score.shape[0])

    def body(_, carry):
        s, sel = carry
        m = jnp.max(s, 0, keepdims=True)
        idx = jnp.min(jnp.where(s == m, row, sentinel), 0, keepdims=True)
        hit = row == idx
        return jnp.where(hit, NEG_INF, s), jnp.where(hit, 1.0, sel)

    _, sel = lax.fori_loop(0, n_pick, body, (score, jnp.zeros_like(score)))
    return sel


def _top_values_t(score, row, n_pick):
    sentinel = float(score.shape[0])
    n = score.shape[1]
    out_row = lax.broadcasted_iota(jnp.int32, (n_pick, n), 0)

    def body(k, carry):
        s, vals, idxs = carry
        m = jnp.max(s, 0, keepdims=True)
        idx = jnp.min(jnp.where(s == m, row, sentinel), 0, keepdims=True)
        here = out_row == k
        return jnp.where(row == idx, NEG_INF, s), jnp.where(here, m, vals), jnp.where(here, idx, idxs)

    zeros = jnp.zeros((n_pick, n), F32)
    _, vals, idxs = lax.fori_loop(0, n_pick, body, (score, zeros, zeros))
    return vals, idxs


def _mm_kernel(a_ref, b_ref, s_ref, o_ref):
    acc = _dot(a_ref[...], b_ref[...])
    o_ref[...] = (acc * s_ref[...]).astype(o_ref.dtype)


def _matmul(a, b, col_scale, out_dtype, tm, tn, name):
    m, k = a.shape
    n = b.shape[1]
    return pl.pallas_call(
        _mm_kernel,
        out_shape=jax.ShapeDtypeStruct((m, n), out_dtype),
        grid=(m // tm, n // tn),
        in_specs=[pl.BlockSpec((tm, k), lambda i, j: (i, 0)),
                  pl.BlockSpec((k, tn), lambda i, j: (0, j)),
                  pl.BlockSpec((1, tn), lambda i, j: (0, j))],
        out_specs=pl.BlockSpec((tm, tn), lambda i, j: (i, j)),
        compiler_params=_cparams("parallel", "parallel"),
        name=name,
    )(a, b, col_scale)


def _proj_t_kernel(w_ref, x_ref, o_ref):
    o_ref[...] = _dot_nt(w_ref[...], x_ref[...]).astype(o_ref.dtype)


def _project_t(w_t, xb, batch, seq, tm):
    n, k = w_t.shape
    per = seq // tm
    return pl.pallas_call(
        _proj_t_kernel,
        out_shape=jax.ShapeDtypeStruct((batch * n, seq), MXU_DTYPE),
        grid=(batch * per,),
        in_specs=[pl.BlockSpec((n, k), lambda i: (0, 0)), pl.BlockSpec((tm, k), lambda i: (i, 0))],
        out_specs=pl.BlockSpec((n, tm), lambda i: (i // per, i % per)),
        compiler_params=_cparams("parallel"),
        name="in_proj_vt",
    )(w_t, xb)


def _compress_kernel(r_ref, pelo_ref, pehi_ref, w1lo_ref, w1hi_ref, w2_ref, o_ref):
    r = r_ref[0]
    ns = r.shape[0]
    lo = _dot((r + pelo_ref[...]).astype(MXU_DTYPE), w1lo_ref[...])
    hi = _dot((r + pehi_ref[...]).astype(MXU_DTYPE), w1hi_ref[...])
    hid = jax.nn.gelu(lo + pltpu.roll(hi, ns - 1, 0))
    o_ref[0] = _dot(hid.astype(MXU_DTYPE), w2_ref[...]).astype(o_ref.dtype)


def _compress(strips, pe, w1, w2):
    bg, ns, width = strips.shape
    half = NSA_CMP_LEN // 2
    pelo = pe[:half].reshape(1, width)
    pehi = pe[half:].reshape(1, width)
    w1lo = w1[:half].reshape(width, HEAD_DIM).astype(MXU_DTYPE)
    w1hi = w1[half:].reshape(width, HEAD_DIM).astype(MXU_DTYPE)
    const = lambda shape: pl.BlockSpec(shape, lambda i: (0,) * len(shape))
    return pl.pallas_call(
        _compress_kernel,
        out_shape=jax.ShapeDtypeStruct((bg, ns, HEAD_DIM), MXU_DTYPE),
        grid=(bg,),
        in_specs=[pl.BlockSpec((1, ns, width), lambda i: (i, 0, 0)),
                  const((1, width)), const((1, width)),
                  const((width, HEAD_DIM)), const((width, HEAD_DIM)), const((HEAD_DIM, HEAD_DIM))],
        out_specs=pl.BlockSpec((1, ns, HEAD_DIM), lambda i: (i, 0, 0)),
        compiler_params=_cparams("parallel"),
        name="nsa_compress",
    )(strips, pelo, pehi, w1lo, w1hi, w2.astype(MXU_DTYPE))


def _nsa_kernel(q_ref, kc_ref, vc_ref, ks_ref, vst_ref, kw_ref, vw_ref, gl_ref, slope_ref, slope_t_ref,
                o_ref, selt_ref, flag_ref, *, top_n):
    c = pl.program_id(2)
    q0 = c * Q_BLK
    rows = NSA_ROWS
    q = q_ref[...]
    qh = jnp.concatenate([q[:, h * HEAD_DIM:(h + 1) * HEAD_DIM] for h in range(NSA_HPG)], axis=0)
    slope = slope_ref[0][:, 0:1]
    slope_t = slope_t_ref[0]

    def tpos(width):
        return q0 + (lax.broadcasted_iota(jnp.int32, (rows, width), 0) & (Q_BLK - 1))

    kc = kc_ref[0]
    ns = kc.shape[0]
    cend = lax.broadcasted_iota(jnp.int32, (rows, ns), 1) * NSA_CMP_STRIDE + (NSA_CMP_LEN - 1)
    t_c = tpos(ns)
    s_c = _dot_nt(qh, kc) - slope * (t_c - cend).astype(F32)
    p_c = _masked_softmax(s_c, cend <= t_c)
    o_c = _dot(p_c.astype(MXU_DTYPE), vc_ref[0])

    p_sum = p_c[0:Q_BLK]
    for h in range(1, NSA_HPG):
        p_sum = p_sum + p_c[h * Q_BLK:(h + 1) * Q_BLK]
    p_two = jnp.concatenate([p_sum, p_sum], axis=0)
    ratio = NSA_SEL_BLOCK // NSA_CMP_STRIDE
    gj = lax.broadcasted_iota(jnp.int32, (LANES, ns), 0) * ratio
    gi = lax.broadcasted_iota(jnp.int32, (LANES, ns), 1)
    gather01 = jnp.where((gi >= gj - 1) & (gi <= gj + ratio - 1), 1.0, 0.0).astype(MXU_DTYPE)
    hi, mid, lo = _split3(p_two)
    imp_t = _dot_nt(gather01, hi) + _dot_nt(gather01, mid) + _dot_nt(gather01, lo)
    jrow_i = lax.broadcasted_iota(jnp.int32, (LANES, LANES), 0)
    forced = (jrow_i == 0) | (jrow_i == c) | (jrow_i == c - 1)
    valid = jrow_i <= c
    score_t = jnp.where(valid, imp_t + jnp.where(forced, NSA_FORCE_BONUS, 0.0), NEG_INF)
    sel_t = jnp.where(valid, _top_select_t(score_t, jrow_i.astype(F32), top_n), 0.0)
    selt_ref[...] = jnp.concatenate([sel_t, sel_t], axis=1)
    picked_rows = jnp.max(sel_t, 1, keepdims=True)
    for kb in range(LANES // SEL_STEP_BLOCKS):
        step_rows = picked_rows[kb * SEL_STEP_BLOCKS:(kb + 1) * SEL_STEP_BLOCKS]
        flag_ref[kb] = (jnp.max(step_rows, 0, keepdims=True)[0, 0] > 0.5).astype(jnp.int32)

    off = (lax.broadcasted_iota(jnp.int32, (SEL_STEP, rows), 0)
           - (lax.broadcasted_iota(jnp.int32, (SEL_STEP, rows), 1) & (Q_BLK - 1)))
    bias_t = slope_t * (-off).astype(F32)

    def step_picks(kb):
        return selt_ref[pl.ds(pl.multiple_of(kb * SEL_STEP_BLOCKS, SEL_STEP_BLOCKS), SEL_STEP_BLOCKS), :]

    def sel_step(kb, carry, causal):
        k0 = pl.multiple_of(kb * SEL_STEP, SEL_STEP)
        picks = step_picks(kb)
        n_half = SEL_STEP // 2
        halves = []
        for hf in range(2):
            kh = pl.multiple_of(k0 + hf * n_half, n_half)
            s = _dot_nt(ks_ref[pl.ds(kh, n_half), :], qh) - bias_t[hf * n_half:(hf + 1) * n_half]
            masked = []
            for i in range(SEL_STEP_BLOCKS // 2):
                r = slice(i * NSA_SEL_BLOCK, (i + 1) * NSA_SEL_BLOCK)
                b = hf * (SEL_STEP_BLOCKS // 2) + i
                ok = picks[b:b + 1, :] > 0.5
                if causal:
                    ok = ok & (off[b * NSA_SEL_BLOCK:(b + 1) * NSA_SEL_BLOCK] <= q0 - k0)
                masked.append(jnp.where(ok, s[r], NEG_INF))
            halves.append((jnp.concatenate(masked, axis=0), vst_ref[:, pl.ds(kh, n_half)]))
        return _online_step_t(carry, halves, slope_t * (q0 - k0).astype(F32))

    def maybe_step(kb, carry):
        return lax.cond(flag_ref[kb] > 0, lambda cr: sel_step(kb, cr, causal=False), lambda cr: cr, carry)

    last = c // SEL_STEP_BLOCKS
    carry = lax.fori_loop(0, last, maybe_step, _flash_init(rows))
    _, l_s, acc_s = sel_step(last, carry, causal=True)
    o_s = (acc_s / jnp.where(l_s > 0, l_s, 1.0)).T

    w0 = pl.multiple_of(jnp.maximum(q0 - (WIN_SPAN - Q_BLK), 0), Q_BLK)
    t_w = tpos(WIN_SPAN)
    dist_w = t_w - (w0 + lax.broadcasted_iota(jnp.int32, (rows, WIN_SPAN), 1))
    s_w = _dot_nt(qh, kw_ref[pl.ds(w0, WIN_SPAN), :]) - slope * dist_w.astype(F32)
    p_w = _masked_softmax(s_w, (dist_w >= 0) & (dist_w < NSA_WINDOW))
    o_w = _dot(p_w.astype(MXU_DTYPE), vw_ref[pl.ds(w0, WIN_SPAN), :])

    gates = 1.0 / (1.0 + jnp.exp(-gl_ref[...]))
    for h in range(NSA_HPG):
        r = slice(h * Q_BLK, (h + 1) * Q_BLK)
        o = (gates[:, 3 * h:3 * h + 1] * o_c[r] + gates[:, 3 * h + 1:3 * h + 2] * o_s[r]
             + gates[:, 3 * h + 2:3 * h + 3] * o_w[r])
        o_ref[:, h * HEAD_DIM:(h + 1) * HEAD_DIM] = o.astype(o_ref.dtype)


def _nsa(proj, aux, kc, vc, vs_t, slopes, slopes_t, batch, seq, cols):
    n_chunks = seq // Q_BLK
    ns = kc.shape[1]
    gw = NSA_HPG * HEAD_DIM
    top_n = min(NSA_TOPN, seq // NSA_SEL_BLOCK)
    kv = lambda base: pl.BlockSpec((seq, HEAD_DIM), lambda b, g, c: (b, base + g))
    cmp_spec = pl.BlockSpec((1, ns, HEAD_DIM), lambda b, g, c: (b * NSA_KV_GROUPS + g, 0, 0))
    return pl.pallas_call(
        functools.partial(_nsa_kernel, top_n=top_n),
        out_shape=jax.ShapeDtypeStruct((batch * seq, NSA_HEADS * HEAD_DIM), MXU_DTYPE),
        grid=(batch, NSA_KV_GROUPS, n_chunks),
        in_specs=[pl.BlockSpec((Q_BLK, gw), lambda b, g, c: (b * n_chunks + c, g)),
                  cmp_spec, cmp_spec,
                  kv(cols["ks"]),
                  pl.BlockSpec((HEAD_DIM, seq), lambda b, g, c: (b * cols["vt_heads"] + cols["vs_t"] + g, 0)),
                  kv(cols["kw"]), kv(cols["vw"]),
                  pl.BlockSpec((Q_BLK, LANES), lambda b, g, c: (b * n_chunks + c, cols["gl_aux"] + g)),
                  pl.BlockSpec((1, NSA_ROWS, LANES), lambda b, g, c: (g, 0, 0)),
                  pl.BlockSpec((1, 1, NSA_ROWS), lambda b, g, c: (g, 0, 0))],
        out_specs=pl.BlockSpec((Q_BLK, gw), lambda b, g, c: (b * n_chunks + c, g)),
        scratch_shapes=[pltpu.VMEM((LANES, NSA_ROWS), F32), pltpu.SMEM((LANES // SEL_STEP_BLOCKS,), jnp.int32)],
        compiler_params=_cparams("parallel", "parallel", "arbitrary"),
        name="nsa_attention",
    )(proj, kc, vc, proj, vs_t, proj, proj, aux, slopes, slopes_t)


def _moba_kernel(q_ref, k_ref, vt_ref, slope_ref, o_ref, kmean_ref, selt_ref, *, top_m):
    cb = pl.program_id(2)
    seq = k_ref.shape[0]
    n_blocks = seq // MOBA_BLOCK
    nq = MOBA_BLOCK

    @pl.when(cb == 0)
    def _():
        kmean_ref[...] = jnp.zeros_like(kmean_ref)
        kf = k_ref[...].astype(F32).reshape(n_blocks, MOBA_BLOCK, HEAD_DIM)
        kmean_ref[0:n_blocks, :] = jnp.sum(kf, axis=1) * (1.0 / MOBA_BLOCK)

    q = q_ref[...]
    slope = slope_ref[0][:, 0:1]
    brow_i = lax.broadcasted_iota(jnp.int32, (LANES, nq), 0)
    gate_t = jnp.where(brow_i < cb, _dot_nt(kmean_ref[...].astype(MXU_DTYPE), q), NEG_INF)
    selt_ref[...] = jnp.where(brow_i < cb, _top_select_t(gate_t, brow_i.astype(F32), top_m), 0.0)

    off = (lax.broadcasted_iota(jnp.int32, (MOBA_STEP, nq), 0)
           - lax.broadcasted_iota(jnp.int32, (MOBA_STEP, nq), 1))
    bias_t = slope * (-off).astype(F32)
    q0 = cb * MOBA_BLOCK

    def past_step(st, carry):
        k0 = pl.multiple_of(st * MOBA_STEP, MOBA_STEP)
        pair = selt_ref[pl.ds(pl.multiple_of((st // 2) * SUBLANES, SUBLANES), SUBLANES), :]
        picks = jnp.where(st % 2 == 0, pair[0:MOBA_STEP_BLOCKS], pair[MOBA_STEP_BLOCKS:2 * MOBA_STEP_BLOCKS])
        n_half = MOBA_STEP // 2
        halves = []
        for hf in range(2):
            kh = pl.multiple_of(k0 + hf * n_half, n_half)
            s = _dot_nt(k_ref[pl.ds(kh, n_half), :], q) - bias_t[hf * n_half:(hf + 1) * n_half]
            masked = []
            for i in range(MOBA_STEP_BLOCKS // 2):
                b = hf * (MOBA_STEP_BLOCKS // 2) + i
                masked.append(jnp.where(picks[b:b + 1, :] > 0.5, s[i * MOBA_BLOCK:(i + 1) * MOBA_BLOCK], NEG_INF))
            halves.append((jnp.concatenate(masked, axis=0), vt_ref[:, pl.ds(kh, n_half)]))
        return _online_step_t(carry, halves, slope * (q0 - k0).astype(F32))

    n_steps = (cb + MOBA_STEP_BLOCKS - 1) // MOBA_STEP_BLOCKS
    carry = lax.fori_loop(0, n_steps, past_step, _flash_init(nq))
    kc0 = pl.multiple_of(q0, MOBA_BLOCK)
    s = _dot_nt(k_ref[pl.ds(kc0, MOBA_BLOCK), :], q) - bias_t[0:MOBA_BLOCK]
    s = jnp.where(off[0:MOBA_BLOCK] <= 0, s, NEG_INF)
    _, l, acc = _online_step_t(carry, [(s, vt_ref[:, pl.ds(kc0, MOBA_BLOCK)])], jnp.zeros((1, 1), F32))
    o_ref[...] = (acc / jnp.where(l > 0, l, 1.0)).T.astype(o_ref.dtype)


def _moba(proj, mv_t, slopes, batch, seq, cols):
    n_blocks = seq // MOBA_BLOCK
    top_m = min(MOBA_TOPK, n_blocks)
    return pl.pallas_call(
        functools.partial(_moba_kernel, top_m=top_m),
        out_shape=jax.ShapeDtypeStruct((batch * seq, MOBA_HEADS * HEAD_DIM), MXU_DTYPE),
        grid=(batch, MOBA_HEADS, n_blocks),
        in_specs=[pl.BlockSpec((MOBA_BLOCK, HEAD_DIM), lambda b, h, c: (b * n_blocks + c, cols["mq"] + h)),
                  pl.BlockSpec((seq, HEAD_DIM), lambda b, h, c: (b, cols["mk"] + h)),
                  pl.BlockSpec((HEAD_DIM, seq), lambda b, h, c: (b * cols["vt_heads"] + cols["mv_t"] + h, 0)),
                  pl.BlockSpec((1, 1, LANES), lambda b, h, c: (h, 0, 0))],
        out_specs=pl.BlockSpec((MOBA_BLOCK, HEAD_DIM), lambda b, h, c: (b * n_blocks + c, h)),
        scratch_shapes=[pltpu.VMEM((LANES, HEAD_DIM), F32), pltpu.VMEM((LANES, MOBA_BLOCK), F32)],
        compiler_params=_cparams("parallel", "parallel", "arbitrary"),
        name="moba_attention",
    )(proj, proj, mv_t, slopes)


def _outproj_ln_kernel(a1_ref, a2_ref, x_ref, w1_ref, w2_ref, g_ref, b_ref, o_ref):
    y = DN_ALPHA * x_ref[...] + _dot(a1_ref[...], w1_ref[...]) + _dot(a2_ref[...], w2_ref[...])
    o_ref[...] = _layernorm(y, g_ref[...], b_ref[...])


def _outproj_ln(a1, a2, x, w1, w2, g, b, tm):
    t, d = x.shape
    k1, k2 = a1.shape[1], a2.shape[1]
    const = lambda shape: pl.BlockSpec(shape, lambda i: (0, 0))
    return pl.pallas_call(
        _outproj_ln_kernel,
        out_shape=jax.ShapeDtypeStruct((t, d), F32),
        grid=(t // tm,),
        in_specs=[pl.BlockSpec((tm, k1), lambda i: (i, 0)), pl.BlockSpec((tm, k2), lambda i: (i, 0)),
                  pl.BlockSpec((tm, d), lambda i: (i, 0)),
                  const((k1, d)), const((k2, d)), const((1, d)), const((1, d))],
        out_specs=pl.BlockSpec((tm, d), lambda i: (i, 0)),
        compiler_params=_cparams("parallel"),
        name="out_proj_ln",
    )(a1, a2, x, w1, w2, g, b)


def _xattn_ln_kernel(h_ref, wq_ref, k_ref, v_ref, wo_ref, g_ref, b_ref, o_ref, o_lowp_ref):
    h = h_ref[...]
    q = (_dot(h.astype(MXU_DTYPE), wq_ref[...]) * (XA_DIM ** -0.5)).astype(MXU_DTYPE)
    outs = []
    for hd in range(XA_HEADS):
        cs = slice(hd * XA_DIM, (hd + 1) * XA_DIM)
        s = _dot_nt(q[:, cs], k_ref[:, cs])
        m = jnp.max(s, -1, keepdims=True)
        e = jnp.exp(s - m)
        p = e / jnp.sum(e, -1, keepdims=True)
        outs.append(_dot(p.astype(MXU_DTYPE), v_ref[:, cs]).astype(MXU_DTYPE))
    o = jnp.concatenate(outs, axis=1)
    y = _layernorm(DN_ALPHA * h + _dot(o, wo_ref[...]), g_ref[...], b_ref[...])
    o_ref[...] = y
    o_lowp_ref[...] = y.astype(o_lowp_ref.dtype)


def _xattn_ln(h, kv, wq, wo, g, b, batch, seq, mem_len, tm):
    t, d = h.shape
    e = XA_HEADS * XA_DIM
    nt = seq // tm
    const = lambda shape: pl.BlockSpec(shape, lambda bi, i: (0, 0))
    return pl.pallas_call(
        _xattn_ln_kernel,
        out_shape=(jax.ShapeDtypeStruct((t, d), F32), jax.ShapeDtypeStruct((t, d), MXU_DTYPE)),
        grid=(batch, nt),
        in_specs=[pl.BlockSpec((tm, d), lambda bi, i: (bi * nt + i, 0)),
                  const((d, e)),
                  pl.BlockSpec((mem_len, e), lambda bi, i: (bi, 0)),
                  pl.BlockSpec((mem_len, e), lambda bi, i: (bi, 1)),
                  const((e, d)), const((1, d)), const((1, d))],
        out_specs=(pl.BlockSpec((tm, d), lambda bi, i: (bi * nt + i, 0)),
                   pl.BlockSpec((tm, d), lambda bi, i: (bi * nt + i, 0))),
        compiler_params=_cparams("parallel", "parallel"),
        name="xattn_ln",
    )(h, wq, kv, kv, wo, g, b)


def _peer_route_kernel(h_ref, wq_ref, sk_ref, gw_ref, ei_ref):
    hh = pl.program_id(1)
    tm = h_ref.shape[0]
    k_top = PEER_TOPK
    q = _dot(h_ref[...], wq_ref[...]).astype(MXU_DTYPE)
    key_row = lax.broadcasted_iota(jnp.int32, (PEER_NKEYS, tm), 0).astype(F32)
    tops = []
    for half in range(2):
        s = _dot_nt(sk_ref[0, half], q[:, half * PEER_NKEYS:(half + 1) * PEER_NKEYS])
        tops.append(_top_values_t(s, key_row, k_top))
    (v0, i0), (v1, i1) = tops

    def grid(a, b):
        first = [a[0:1] + b]
        mid = [a[i:i + 1] + b[0:SUBLANES] for i in range(1, SUBLANES)]
        return jnp.concatenate(first + mid + [a[SUBLANES:] + b[0:1]], axis=0)

    cand = grid(v0, v1)
    cidx = grid(i0 * float(PEER_NKEYS), i1)
    n_cand = cand.shape[0]
    cand_row = lax.broadcasted_iota(jnp.int32, (n_cand, tm), 0).astype(F32)
    pick_row = lax.broadcasted_iota(jnp.int32, (k_top, tm), 0)

    def pick_expert(k, carry):
        cd, tv, te = carry
        m = jnp.max(cd, 0, keepdims=True)
        j = jnp.min(jnp.where(cd == m, cand_row, float(n_cand)), 0, keepdims=True)
        hit = cand_row == j
        e = jnp.sum(jnp.where(hit, cidx, 0.0), 0, keepdims=True)
        here = pick_row == k
        return jnp.where(hit, NEG_INF, cd), jnp.where(here, m, tv), jnp.where(here, e, te)

    zeros = jnp.zeros((k_top, tm), F32)
    _, tv, te = lax.fori_loop(0, k_top, pick_expert, (cand, zeros, zeros))
    ex = jnp.exp(tv - jnp.max(tv, 0, keepdims=True))
    rows = pl.ds(pl.multiple_of(hh * k_top, k_top), k_top)
    gw_ref[rows, :] = ex / jnp.sum(ex, 0, keepdims=True)
    ei_ref[rows, :] = te.astype(jnp.int32)


def _peer_route(h, wq, sub_keys, tm):
    t, d = h.shape
    return pl.pallas_call(
        _peer_route_kernel,
        out_shape=(jax.ShapeDtypeStruct((PEER_PICKS, t), F32), jax.ShapeDtypeStruct((PEER_PICKS, t), jnp.int32)),
        grid=(t // tm, PEER_HEADS),
        in_specs=[pl.BlockSpec((tm, d), lambda i, hh: (i, 0)),
                  pl.BlockSpec((d, PEER_QDIM), lambda i, hh: (0, hh)),
                  pl.BlockSpec((1, 2, PEER_NKEYS, PEER_QDIM // 2), lambda i, hh: (hh, 0, 0, 0))],
        out_specs=(pl.BlockSpec((PEER_PICKS, tm), lambda i, hh: (0, i)),
                   pl.BlockSpec((PEER_PICKS, tm), lambda i, hh: (0, i))),
        compiler_params=_cparams("parallel", "arbitrary"),
        name="peer_route",
    )(h, wq, sub_keys)


def _peer_expert_kernel(ei_ref, gw_ref, x_ref, uv_ref, o_ref, *scratch):
    bufs, sem_ref = scratch[:PEER_SLOTS], scratch[PEER_SLOTS]
    n_tok = x_ref.shape[0]
    half = PEER_SLAB // 2
    groups = PEER_PICKS // SUBLANES
    lookahead = PEER_SLOTS - 1

    def slab(slot, j):
        return bufs[slot].at[pl.ds(j * PEER_SLAB_PITCH, PEER_SLAB)]

    def issue(tok, slot):
        for j in range(PEER_PICKS):
            pltpu.make_async_copy(uv_ref.at[ei_ref[j, tok]], slab(slot, j), sem_ref.at[slot]).start(priority=j % 2)

    def wait(slot):
        for j in range(PEER_PICKS):
            pltpu.make_async_copy(uv_ref.at[0], slab(slot, j), sem_ref.at[slot]).wait()

    gw_t = gw_ref[...]
    tok_lane = lax.broadcasted_iota(jnp.int32, gw_t.shape, 1)

    def compute(tok, slot):
        def words(j0, r):
            w = bufs[slot][pl.ds(j0 * PEER_SLAB_PITCH + r, SUBLANES, stride=PEER_SLAB_PITCH), :]
            return (lax.bitcast_convert_type(w << 16, F32),
                    lax.bitcast_convert_type(w & jnp.uint32(0xFFFF0000), F32))

        xs = [x_ref[tok, pl.ds(r, 1), :] for r in range(2 * half)]
        parts = []
        for jg in range(groups):
            acc = None
            for r in range(half):
                lo, hi = words(jg * SUBLANES, r)
                term = lo * xs[r] + hi * xs[half + r]
                acc = term if acc is None else acc + term
            parts.append(jnp.sum(acc, -1, keepdims=True))
        a = jnp.concatenate(parts, axis=0)
        gw_col = jnp.sum(jnp.where(tok_lane == tok, gw_t, 0.0), -1, keepdims=True)
        w = gw_col * jax.nn.gelu(a)
        ws = [w[jg * SUBLANES:(jg + 1) * SUBLANES] for jg in range(groups)]
        for r in range(half):
            acc_lo = acc_hi = None
            for jg in range(groups):
                lo, hi = words(jg * SUBLANES, half + r)
                acc_lo = lo * ws[jg] if acc_lo is None else acc_lo + lo * ws[jg]
                acc_hi = hi * ws[jg] if acc_hi is None else acc_hi + hi * ws[jg]
            o_ref[tok, pl.ds(r, 1), :] = jnp.sum(acc_lo, 0, keepdims=True)
            o_ref[tok, pl.ds(half + r, 1), :] = jnp.sum(acc_hi, 0, keepdims=True)

    def round_of_slots(base, n_issue):
        for s in range(PEER_SLOTS):
            wait(s)
            if s < n_issue:
                issue(base + s + lookahead, (s + lookahead) % PEER_SLOTS)
            compute(base + s, s)

    for s in range(lookahead):
        issue(s, s)
    n_rounds = n_tok // PEER_SLOTS

    def round_body(i, _):
        round_of_slots(i * PEER_SLOTS, PEER_SLOTS)
        return 0

    lax.fori_loop(0, n_rounds - 1, round_body, 0)
    round_of_slots((n_rounds - 1) * PEER_SLOTS, PEER_SLOTS - lookahead)


def _peer_experts(eidx_t, gw_t, x3, uv):
    t = x3.shape[0]
    tt = PEER_TOK_TILE
    return pl.pallas_call(
        _peer_expert_kernel,
        out_shape=jax.ShapeDtypeStruct((t, PEER_SLAB, LANES), F32),
        grid=(t // tt,),
        in_specs=[pl.BlockSpec((PEER_PICKS, tt), lambda i: (0, i), memory_space=pltpu.SMEM),
                  pl.BlockSpec((PEER_PICKS, tt), lambda i: (0, i)),
                  pl.BlockSpec((tt, PEER_SLAB, LANES), lambda i: (i, 0, 0)),
                  pl.BlockSpec(memory_space=pl.ANY)],
        out_specs=pl.BlockSpec((tt, PEER_SLAB, LANES), lambda i: (i, 0, 0)),
        scratch_shapes=[pltpu.VMEM((PEER_PICKS * PEER_SLAB_PITCH, LANES), jnp.uint32) for _ in range(PEER_SLOTS)]
                       + [pltpu.SemaphoreType.DMA((PEER_SLOTS,))],
        compiler_params=_cparams("arbitrary"),
        name="peer_experts",
    )(eidx_t, gw_t, x3, uv)


def _add_ln_kernel(h_ref, f_ref, g_ref, b_ref, o_ref):
    o_ref[...] = _layernorm(DN_ALPHA * h_ref[...] + f_ref[...], g_ref[...], b_ref[...])


def _add_ln(h, f, g, b, tm):
    t, d = h.shape
    row = pl.BlockSpec((tm, d), lambda i: (i, 0))
    const = pl.BlockSpec((1, d), lambda i: (0, 0))
    return pl.pallas_call(
        _add_ln_kernel,
        out_shape=jax.ShapeDtypeStruct((t, d), F32),
        grid=(t // tm,),
        in_specs=[row, row, const, const],
        out_specs=row,
        compiler_params=_cparams("parallel"),
        name="add_ln",
    )(h, f, g, b)


def _alibi_slopes():
    s = (2.0 ** (-8.0 * (np.arange(N_MIX_HEADS) + 1) / N_MIX_HEADS)).astype(np.float32)
    return s[0::2], s[1::2]


def _mixer(x2, batch, seq, w_in, pe_k, w1_k, w2_k, pe_v, w1_v, w2_v):
    hd = HEAD_DIM
    sizes = [NSA_HEADS * hd] + [NSA_KV_GROUPS * hd] * 6 + [NSA_HEADS * 3] + [MOBA_HEADS * hd] * 3
    offs = np.concatenate([[0], np.cumsum(sizes)])
    sec = {n: w_in[:, offs[i]:offs[i + 1]] for i, n in enumerate(
        ["nq", "kc", "vc", "ks", "vs", "kw", "vw", "gl", "mq", "mk", "mv"])}
    order = ["nq", "ks", "kw", "vw", "mq", "mk"]
    cols, at = {}, 0
    for n in order:
        cols[n] = at // LANES
        at += sec[n].shape[1]
    w_main = jnp.concatenate([sec[n] for n in order], axis=1).astype(MXU_DTYPE)
    scale = hd ** -0.5
    col_scale = jnp.concatenate([
        jnp.full((1, sec[n].shape[1]), scale if n in ("nq", "mq") else 1.0, F32) for n in order], axis=1)
    per_group = NSA_HPG * 3
    gl_cols = [jnp.pad(sec["gl"][:, g * per_group:(g + 1) * per_group], ((0, 0), (0, LANES - per_group)))
               for g in range(NSA_KV_GROUPS)]
    w_aux = jnp.concatenate([sec["kc"], sec["vc"]] + gl_cols, axis=1).astype(MXU_DTYPE)
    cols["gl_aux"] = (2 * NSA_KV_GROUPS * hd) // LANES

    xb = x2.astype(MXU_DTYPE)
    tm = 512 if x2.shape[0] % 512 == 0 else 256
    proj = _matmul(xb, w_main, col_scale, MXU_DTYPE, tm, 768, "in_proj")
    w_vt = jnp.concatenate([sec["vs"], sec["mv"]], axis=1).T.astype(MXU_DTYPE)
    v_t = _project_t(w_vt, xb, batch, seq, tm)
    cols["vs_t"], cols["mv_t"], cols["vt_heads"] = 0, NSA_KV_GROUPS, NSA_KV_GROUPS + MOBA_HEADS
    aux = _matmul(xb, w_aux, jnp.ones((1, w_aux.shape[1]), F32), F32, tm, 256, "in_proj_aux")

    ns = seq // NSA_CMP_STRIDE

    def strips(col0):
        raw = aux[:, col0:col0 + NSA_KV_GROUPS * hd].reshape(batch, seq, NSA_KV_GROUPS, hd)
        return raw.transpose(0, 2, 1, 3).reshape(batch * NSA_KV_GROUPS, ns, NSA_CMP_STRIDE * hd)

    kc = _compress(strips(0), pe_k, w1_k, w2_k)
    vc = _compress(strips(NSA_KV_GROUPS * hd), pe_v, w1_v, w2_v)

    slope_n, slope_m = _alibi_slopes()
    sn = np.repeat(slope_n.reshape(NSA_KV_GROUPS, NSA_HPG), Q_BLK, axis=1)
    sn_rows = jnp.asarray(np.broadcast_to(sn[:, :, None], sn.shape + (LANES,)).copy())
    sn_lanes = jnp.asarray(sn[:, None, :].copy())
    sm = jnp.asarray(np.broadcast_to(slope_m[:, None, None], (MOBA_HEADS, 1, LANES)).copy())

    o_nsa = _nsa(proj, aux, kc, vc, v_t, sn_rows, sn_lanes, batch, seq, cols)
    o_moba = _moba(proj, v_t, sm, batch, seq, cols)
    return o_nsa, o_moba


def _memory_xattn_ln(h, mem2, batch, seq, wq, wkv, wo, g, b):
    mem_len = mem2.shape[0] // batch
    e2 = wkv.shape[1]
    kv = _matmul(mem2.astype(MXU_DTYPE), wkv.astype(MXU_DTYPE), jnp.ones((1, e2), F32), MXU_DTYPE,
                 mem_len, e2 // 2, "xattn_kv")
    return _xattn_ln(h, kv, wq.astype(MXU_DTYPE), wo.astype(MXU_DTYPE), g, b, batch, seq, mem_len, 256)


def _pack_tables_kernel(u_ref, v_ref, o_ref):
    def words(x):
        bits = lax.bitcast_convert_type(x.astype(jnp.bfloat16).astype(F32), jnp.uint32)
        half = x.shape[1] // 2
        return (bits[:, :half] >> 16) | (bits[:, half:] & jnp.uint32(0xFFFF0000))

    n = u_ref.shape[0]
    per_table = PEER_SLAB // 2
    for t, ref in enumerate((u_ref, v_ref)):
        w = words(ref[...])
        for r in range(per_table):
            o_ref[pl.ds(t * per_table + r, n, stride=PEER_SLAB), :] = w[:, r * LANES:(r + 1) * LANES]


def _pack_expert_tables(exp_u, exp_v):
    n_exp, d = exp_u.shape
    blk = 256
    packed = pl.pallas_call(
        _pack_tables_kernel,
        out_shape=jax.ShapeDtypeStruct((n_exp * PEER_SLAB, LANES), jnp.uint32),
        grid=(n_exp // blk,),
        in_specs=[pl.BlockSpec((blk, d), lambda i: (i, 0)), pl.BlockSpec((blk, d), lambda i: (i, 0))],
        out_specs=pl.BlockSpec((blk * PEER_SLAB, LANES), lambda i: (i, 0)),
        compiler_params=_cparams("parallel"),
        name="peer_pack_tables",
    )(exp_u, exp_v)
    return packed.reshape(n_exp, PEER_SLAB, LANES)


def _peer_ln(h, h_lowp, wq, sub_keys, exp_u, exp_v, g, b):
    t, d = h.shape
    gw_t, eidx_t = _peer_route(h_lowp, wq.astype(MXU_DTYPE), sub_keys.astype(MXU_DTYPE), 256)
    uv = _pack_expert_tables(exp_u, exp_v)
    f = _peer_experts(eidx_t, gw_t, h.reshape(t, PEER_SLAB, LANES), uv).reshape(t, d)
    return _add_ln(h, f, g, b, 256)


def kernel(x, mem, w_in, cmp_pe_k, cmp_w1_k, cmp_w2_k, cmp_pe_v, cmp_w1_v, cmp_w2_v, w_out, ln1_g, ln1_b,
           xa_wq, xa_wkv, xa_wo, ln2_g, ln2_b, peer_wq, peer_subkeys, peer_u, peer_v, ln3_g, ln3_b):
    batch, seq, d = x.shape
    assert seq % MOBA_STEP == 0 and WIN_SPAN <= seq <= NSA_SEL_BLOCK * LANES
    assert d == PEER_SLAB * LANES and w_in.shape[0] == DEPTH and PEER_TOPK == 2 * SUBLANES
    row = lambda v: v.reshape(1, d)
    h = x.reshape(batch * seq, d)
    mem2 = mem.reshape(-1, d)
    for l in range(DEPTH):
        o_nsa, o_moba = _mixer(h, batch, seq, w_in[l], cmp_pe_k[l], cmp_w1_k[l], cmp_w2_k[l],
                               cmp_pe_v[l], cmp_w1_v[l], cmp_w2_v[l])
        wo = w_out[l].astype(MXU_DTYPE)
        k1 = o_nsa.shape[1]
        h = _outproj_ln(o_nsa, o_moba, h, wo[:k1], wo[k1:], row(ln1_g[l]), row(ln1_b[l]), 256)
        h, h_lowp = _memory_xattn_ln(h, mem2, batch, seq, xa_wq[l], xa_wkv[l], xa_wo[l],
                                     row(ln2_g[l]), row(ln2_b[l]))
        h = _peer_ln(h, h_lowp, peer_wq[l], peer_subkeys[l], peer_u[l], peer_v[l], row(ln3_g[l]), row(ln3_b[l]))
    return h.reshape(batch, seq, d)
```

```python
import functools

import numpy as np
import jax
import jax.numpy as jnp
from jax import lax
from jax.experimental import pallas as pl
from jax.experimental.pallas import tpu as pltpu

F32 = jnp.float32
MXU_DTYPE = jnp.bfloat16
NEG_INF = float("-inf")

LANES = 128
SUBLANES = 8
VMEM_LIMIT = 48 * 1024 * 1024

HEAD_DIM = 128
N_MIX_HEADS = 16
NSA_HEADS = 8
NSA_KV_GROUPS = 2
NSA_HPG = NSA_HEADS // NSA_KV_GROUPS
NSA_CMP_STRIDE = 16
NSA_CMP_LEN = 32
NSA_SEL_BLOCK = 64
NSA_TOPN = 16
NSA_WINDOW = 512
NSA_FORCE_BONUS = 1.0e4
MOBA_HEADS = 8
MOBA_BLOCK = 256
MOBA_TOPK = 3
Q_BLK = 64
XA_HEADS = 4
XA_DIM = 128
PEER_HEADS = 8
PEER_NKEYS = 128
PEER_QDIM = 256
PEER_TOPK = 16
DEPTH = 1
DN_ALPHA = (2 * DEPTH) ** 0.25
LN_EPS = 1e-5

NSA_ROWS = NSA_HPG * Q_BLK
WIN_SPAN = NSA_WINDOW + 2 * Q_BLK
SEL_STEP_BLOCKS = 2 * SUBLANES
SEL_STEP = SEL_STEP_BLOCKS * NSA_SEL_BLOCK
MOBA_STEP_BLOCKS = 4
MOBA_STEP = MOBA_STEP_BLOCKS * MOBA_BLOCK
PEER_PICKS = PEER_HEADS * PEER_TOPK
PEER_SLAB = 16
PEER_SLAB_PITCH = 24
PEER_SLOTS = 8
PEER_TOK_TILE = LANES


def _cparams(*sem):
    return pltpu.CompilerParams(dimension_semantics=sem, vmem_limit_bytes=VMEM_LIMIT)


def _dot(a, b):
    return jnp.dot(a, b, preferred_element_type=F32)


def _dot_nt(a, b):
    return lax.dot_general(a, b, (((1,), (1,)), ((), ())), preferred_element_type=F32)


def _split3(x):
    hi = x.astype(MXU_DTYPE)
    r1 = x - hi.astype(F32)
    mid = r1.astype(MXU_DTYPE)
    lo = (r1 - mid.astype(F32)).astype(MXU_DTYPE)
    return hi, mid, lo


def _masked_softmax(s, mask):
    s = jnp.where(mask, s, NEG_INF)
    m = jnp.max(s, -1, keepdims=True)
    m = jnp.where(jnp.isfinite(m), m, 0.0)
    e = jnp.where(mask, jnp.exp(s - m), 0.0)
    d = jnp.sum(e, -1, keepdims=True)
    return e / jnp.where(d > 0, d, 1.0)


def _flash_partial(s, row_shift, vt):
    m = jnp.max(s, 0, keepdims=True)
    p = jnp.exp(s - jnp.where(m == NEG_INF, 0.0, m))
    return m - row_shift, jnp.sum(p, 0, keepdims=True), _dot(vt, p.astype(MXU_DTYPE))


def _online_step_t(carry, pieces, row_shift):
    m_i, l_i, acc = carry
    parts = [_flash_partial(s, row_shift, vt) for s, vt in pieces]
    m_new = m_i
    for m_p, _, _ in parts:
        m_new = jnp.maximum(m_new, m_p)
    m_safe = jnp.where(m_new == NEG_INF, 0.0, m_new)
    alpha = jnp.exp(m_i - m_safe)
    l_new, acc_new = alpha * l_i, alpha * acc
    for m_p, l_p, acc_p in parts:
        alpha = jnp.exp(m_p - m_safe)
        l_new, acc_new = l_new + alpha * l_p, acc_new + alpha * acc_p
    return m_new, l_new, acc_new


def _flash_init(n_q):
    return (jnp.full((1, n_q), NEG_INF, F32), jnp.zeros((1, n_q), F32), jnp.zeros((HEAD_DIM, n_q), F32))


def _layernorm(y, g, b):
    mu = jnp.mean(y, -1, keepdims=True)
    var = jnp.mean(jnp.square(y - mu), -1, keepdims=True)
    return (y - mu) * lax.rsqrt(var + LN_EPS) * g + b


def _top_select_t(score, row, n_pick):
    sentinel = float(score.shape[0])

    def body(_, carry):
        s, sel = carry
        m = jnp.max(s, 0, keepdims=True)
        idx = jnp.min(jnp.where(s == m, row, sentinel), 0, keepdims=True)
        hit = row == idx
        return jnp.where(hit, NEG_INF, s), jnp.where(hit, 1.0, sel)

    _, sel = lax.fori_loop(0, n_pick, body, (score, jnp.zeros_like(score)))
    return sel


def _top_values_t(score, row, n_pick):
    sentinel = float(score.shape[0])
    n = score.shape[1]
    out_row = lax.broadcasted_iota(jnp.int32, (n_pick, n), 0)

    def body(k, carry):
        s, vals, idxs = carry
        m = jnp.max(s, 0, keepdims=True)
        idx = jnp.min(jnp.where(s == m, row, sentinel), 0, keepdims=True)
        here = out_row == k
        return jnp.where(row == idx, NEG_INF, s), jnp.where(here, m, vals), jnp.where(here, idx, idxs)

    zeros = jnp.zeros((n_pick, n), F32)
    _, vals, idxs = lax.fori_loop(0, n_pick, body, (score, zeros, zeros))
    return vals, idxs


def _mm_kernel(a_ref, b_ref, s_ref, o_ref):
    acc = _dot(a_ref[...], b_ref[...])
    o_ref[...] = (acc * s_ref[...]).astype(o_ref.dtype)


def _matmul(a, b, col_scale, out_dtype, tm, tn, name):
    m, k = a.shape
    n = b.shape[1]
    return pl.pallas_call(
        _mm_kernel,
        out_shape=jax.ShapeDtypeStruct((m, n), out_dtype),
        grid=(m // tm, n // tn),
        in_specs=[pl.BlockSpec((tm, k), lambda i, j: (i, 0)),
                  pl.BlockSpec((k, tn), lambda i, j: (0, j)),
                  pl.BlockSpec((1, tn), lambda i, j: (0, j))],
        out_specs=pl.BlockSpec((tm, tn), lambda i, j: (i, j)),
        compiler_params=_cparams("parallel", "parallel"),
        name=name,
    )(a, b, col_scale)


def _proj_t_kernel(w_ref, x_ref, o_ref):
    o_ref[...] = _dot_nt(w_ref[...], x_ref[...]).astype(o_ref.dtype)


def _project_t(w_t, xb, batch, seq, tm):
    n, k = w_t.shape
    per = seq // tm
    return pl.pallas_call(
        _proj_t_kernel,
        out_shape=jax.ShapeDtypeStruct((batch * n, seq), MXU_DTYPE),
        grid=(batch * per,),
        in_specs=[pl.BlockSpec((n, k), lambda i: (0, 0)), pl.BlockSpec((tm, k), lambda i: (i, 0))],
        out_specs=pl.BlockSpec((n, tm), lambda i: (i // per, i % per)),
        compiler_params=_cparams("parallel"),
        name="in_proj_vt",
    )(w_t, xb)


def _compress_kernel(r_ref, pelo_ref, pehi_ref, w1lo_ref, w1hi_ref, w2_ref, o_ref):
    r = r_ref[0]
    ns = r.shape[0]
    lo = _dot((r + pelo_ref[...]).astype(MXU_DTYPE), w1lo_ref[...])
    hi = _dot((r + pehi_ref[...]).astype(MXU_DTYPE), w1hi_ref[...])
    hid = jax.nn.gelu(lo + pltpu.roll(hi, ns - 1, 0))
    o_ref[0] = _dot(hid.astype(MXU_DTYPE), w2_ref[...]).astype(o_ref.dtype)


def _compress(strips, pe, w1, w2):
    bg, ns, width = strips.shape
    half = NSA_CMP_LEN // 2
    pelo = pe[:half].reshape(1, width)
    pehi = pe[half:].reshape(1, width)
    w1lo = w1[:half].reshape(width, HEAD_DIM).astype(MXU_DTYPE)
    w1hi = w1[half:].reshape(width, HEAD_DIM).astype(MXU_DTYPE)
    const = lambda shape: pl.BlockSpec(shape, lambda i: (0,) * len(shape))
    return pl.pallas_call(
        _compress_kernel,
        out_shape=jax.ShapeDtypeStruct((bg, ns, HEAD_DIM), MXU_DTYPE),
        grid=(bg,),
        in_specs=[pl.BlockSpec((1, ns, width), lambda i: (i, 0, 0)),
                  const((1, width)), const((1, width)),
                  const((width, HEAD_DIM)), const((width, HEAD_DIM)), const((HEAD_DIM, HEAD_DIM))],
        out_specs=pl.BlockSpec((1, ns, HEAD_DIM), lambda i: (i, 0, 0)),
        compiler_params=_cparams("parallel"),
        name="nsa_compress",
    )(strips, pelo, pehi, w1lo, w1hi, w2.astype(MXU_DTYPE))


def _nsa_kernel(q_ref, kc_ref, vc_ref, ks_ref, vst_ref, kw_ref, vw_ref, gl_ref, slope_ref, slope_t_ref,
                off_ref, bias_ref, o_ref, selt_ref, flag_ref, *, top_n):
    c = pl.program_id(2)
    q0 = c * Q_BLK
    rows = NSA_ROWS
    q = q_ref[...]
    qh = jnp.concatenate([q[:, h * HEAD_DIM:(h + 1) * HEAD_DIM] for h in range(NSA_HPG)], axis=0)
    slope = slope_ref[0][:, 0:1]
    slope_t = slope_t_ref[0]

    def tpos(width):
        return q0 + (lax.broadcasted_iota(jnp.int32, (rows, width), 0) & (Q_BLK - 1))

    kc = kc_ref[0]
    ns = kc.shape[0]
    cend = lax.broadcasted_iota(jnp.int32, (rows, ns), 1) * NSA_CMP_STRIDE + (NSA_CMP_LEN - 1)
    t_c = tpos(ns)
    s_c = _dot_nt(qh, kc) - slope * (t_c - cend).astype(F32)
    p_c = _masked_softmax(s_c, cend <= t_c)
    o_c = _dot(p_c.astype(MXU_DTYPE), vc_ref[0])

    p_sum = p_c[0:Q_BLK]
    for h in range(1, NSA_HPG):
        p_sum = p_sum + p_c[h * Q_BLK:(h + 1) * Q_BLK]
    p_two = jnp.concatenate([p_sum, p_sum], axis=0)
    ratio = NSA_SEL_BLOCK // NSA_CMP_STRIDE
    gj = lax.broadcasted_iota(jnp.int32, (LANES, ns), 0) * ratio
    gi = lax.broadcasted_iota(jnp.int32, (LANES, ns), 1)
    gather01 = jnp.where((gi >= gj - 1) & (gi <= gj + ratio - 1), 1.0, 0.0).astype(MXU_DTYPE)
    hi, mid, lo = _split3(p_two)
    imp_t = _dot_nt(gather01, hi) + _dot_nt(gather01, mid) + _dot_nt(gather01, lo)
    jrow_i = lax.broadcasted_iota(jnp.int32, (LANES, LANES), 0)
    forced = (jrow_i == 0) | (jrow_i == c) | (jrow_i == c - 1)
    valid = jrow_i <= c
    score_t = jnp.where(valid, imp_t + jnp.where(forced, NSA_FORCE_BONUS, 0.0), NEG_INF)
    sel_t = jnp.where(valid, _top_select_t(score_t, jrow_i.astype(F32), top_n), 0.0)
    selt_ref[...] = jnp.concatenate([sel_t, sel_t], axis=1)
    picked_rows = jnp.max(sel_t, 1, keepdims=True)
    for kb in range(LANES // SEL_STEP_BLOCKS):
        step_rows = picked_rows[kb * SEL_STEP_BLOCKS:(kb + 1) * SEL_STEP_BLOCKS]
        flag_ref[kb] = (jnp.max(step_rows, 0, keepdims=True)[0, 0] > 0.5).astype(jnp.int32)

    bias_ref = bias_ref.at[0]

    def step_picks(kb):
        return selt_ref[pl.ds(pl.multiple_of(kb * SEL_STEP_BLOCKS, SEL_STEP_BLOCKS), SEL_STEP_BLOCKS), :]

    def sel_step(kb, carry, causal):
        k0 = pl.multiple_of(kb * SEL_STEP, SEL_STEP)
        picks = step_picks(kb)
        n_half = SEL_STEP // 2
        halves = []
        for hf in range(2):
            kh = pl.multiple_of(k0 + hf * n_half, n_half)
            s = _dot_nt(ks_ref[pl.ds(kh, n_half), :], qh) - bias_ref[hf * n_half:(hf + 1) * n_half, :]
            masked = []
            for i in range(SEL_STEP_BLOCKS // 2):
                r = slice(i * NSA_SEL_BLOCK, (i + 1) * NSA_SEL_BLOCK)
                b = hf * (SEL_STEP_BLOCKS // 2) + i
                ok = picks[b:b + 1, :] > 0.5
                if causal:
                    ok = ok & (off_ref[b * NSA_SEL_BLOCK:(b + 1) * NSA_SEL_BLOCK, :] <= q0 - k0)
                masked.append(jnp.where(ok, s[r], NEG_INF))
            halves.append((jnp.concatenate(masked, axis=0), vst_ref[:, pl.ds(kh, n_half)]))
        return _online_step_t(carry, halves, slope_t * (q0 - k0).astype(F32))

    def maybe_step(kb, carry):
        return lax.cond(flag_ref[kb] > 0, lambda cr: sel_step(kb, cr, causal=False), lambda cr: cr, carry)

    last = c // SEL_STEP_BLOCKS
    carry = lax.fori_loop(0, last, maybe_step, _flash_init(rows))
    _, l_s, acc_s = sel_step(last, carry, causal=True)
    o_s = (acc_s / jnp.where(l_s > 0, l_s, 1.0)).T

    w0 = pl.multiple_of(jnp.maximum(q0 - (WIN_SPAN - Q_BLK), 0), Q_BLK)
    t_w = tpos(WIN_SPAN)
    dist_w = t_w - (w0 + lax.broadcasted_iota(jnp.int32, (rows, WIN_SPAN), 1))
    s_w = _dot_nt(qh, kw_ref[pl.ds(w0, WIN_SPAN), :]) - slope * dist_w.astype(F32)
    p_w = _masked_softmax(s_w, (dist_w >= 0) & (dist_w < NSA_WINDOW))
    o_w = _dot(p_w.astype(MXU_DTYPE), vw_ref[pl.ds(w0, WIN_SPAN), :])

    gates = 1.0 / (1.0 + jnp.exp(-gl_ref[...]))
    for h in range(NSA_HPG):
        r = slice(h * Q_BLK, (h + 1) * Q_BLK)
        o = (gates[:, 3 * h:3 * h + 1] * o_c[r] + gates[:, 3 * h + 1:3 * h + 2] * o_s[r]
             + gates[:, 3 * h + 2:3 * h + 3] * o_w[r])
        o_ref[:, h * HEAD_DIM:(h + 1) * HEAD_DIM] = o.astype(o_ref.dtype)


def _nsa(proj, aux, kc, vc, vs_t, slopes, slopes_t, off, bias, batch, seq, cols):
    n_chunks = seq // Q_BLK
    ns = kc.shape[1]
    gw = NSA_HPG * HEAD_DIM
    top_n = min(NSA_TOPN, seq // NSA_SEL_BLOCK)
    kv = lambda base: pl.BlockSpec((seq, HEAD_DIM), lambda b, g, c: (b, base + g))
    cmp_spec = pl.BlockSpec((1, ns, HEAD_DIM), lambda b, g, c: (b * NSA_KV_GROUPS + g, 0, 0))
    return pl.pallas_call(
        functools.partial(_nsa_kernel, top_n=top_n),
        out_shape=jax.ShapeDtypeStruct((batch * seq, NSA_HEADS * HEAD_DIM), MXU_DTYPE),
        grid=(batch, NSA_KV_GROUPS, n_chunks),
        in_specs=[pl.BlockSpec((Q_BLK, gw), lambda b, g, c: (b * n_chunks + c, g)),
                  cmp_spec, cmp_spec,
                  kv(cols["ks"]),
                  pl.BlockSpec((HEAD_DIM, seq), lambda b, g, c: (b * cols["vt_heads"] + cols["vs_t"] + g, 0)),
                  kv(cols["kw"]), kv(cols["vw"]),
                  pl.BlockSpec((Q_BLK, LANES), lambda b, g, c: (b * n_chunks + c, cols["gl_aux"] + g)),
                  pl.BlockSpec((1, NSA_ROWS, LANES), lambda b, g, c: (g, 0, 0)),
                  pl.BlockSpec((1, 1, NSA_ROWS), lambda b, g, c: (g, 0, 0)),
                  pl.BlockSpec((SEL_STEP, NSA_ROWS), lambda b, g, c: (0, 0)),
                  pl.BlockSpec((1, SEL_STEP, NSA_ROWS), lambda b, g, c: (g, 0, 0))],
        out_specs=pl.BlockSpec((Q_BLK, gw), lambda b, g, c: (b * n_chunks + c, g)),
        scratch_shapes=[pltpu.VMEM((LANES, NSA_ROWS), F32), pltpu.SMEM((LANES // SEL_STEP_BLOCKS,), jnp.int32)],
        compiler_params=_cparams("parallel", "parallel", "arbitrary"),
        name="nsa_attention",
    )(proj, kc, vc, proj, vs_t, proj, proj, aux, slopes, slopes_t, off, bias)


def _moba_kernel(q_ref, k_ref, vt_ref, slope_ref, off_ref, bias_ref, o_ref, kmean_ref, selt_ref, *, top_m):
    cb = pl.program_id(2)
    seq = k_ref.shape[0]
    n_blocks = seq // MOBA_BLOCK
    nq = MOBA_BLOCK

    @pl.when(cb == 0)
    def _():
        kmean_ref[...] = jnp.zeros_like(kmean_ref)
        kf = k_ref[...].astype(F32).reshape(n_blocks, MOBA_BLOCK, HEAD_DIM)
        kmean_ref[0:n_blocks, :] = jnp.sum(kf, axis=1) * (1.0 / MOBA_BLOCK)

    q = q_ref[...]
    slope = slope_ref[0][:, 0:1]
    brow_i = lax.broadcasted_iota(jnp.int32, (LANES, nq), 0)
    gate_t = jnp.where(brow_i < cb, _dot_nt(kmean_ref[...].astype(MXU_DTYPE), q), NEG_INF)
    selt_ref[...] = jnp.where(brow_i < cb, _top_select_t(gate_t, brow_i.astype(F32), top_m), 0.0)

    bias_ref = bias_ref.at[0]
    q0 = cb * MOBA_BLOCK

    def past_step(st, carry):
        k0 = pl.multiple_of(st * MOBA_STEP, MOBA_STEP)
        pair = selt_ref[pl.ds(pl.multiple_of((st // 2) * SUBLANES, SUBLANES), SUBLANES), :]
        picks = jnp.where(st % 2 == 0, pair[0:MOBA_STEP_BLOCKS], pair[MOBA_STEP_BLOCKS:2 * MOBA_STEP_BLOCKS])
        n_half = MOBA_STEP // 2
        halves = []
        for hf in range(2):
            kh = pl.multiple_of(k0 + hf * n_half, n_half)
            s = _dot_nt(k_ref[pl.ds(kh, n_half), :], q) - bias_ref[hf * n_half:(hf + 1) * n_half, :]
            masked = []
            for i in range(MOBA_STEP_BLOCKS // 2):
                b = hf * (MOBA_STEP_BLOCKS // 2) + i
                masked.append(jnp.where(picks[b:b + 1, :] > 0.5, s[i * MOBA_BLOCK:(i + 1) * MOBA_BLOCK], NEG_INF))
            halves.append((jnp.concatenate(masked, axis=0), vt_ref[:, pl.ds(kh, n_half)]))
        return _online_step_t(carry, halves, slope * (q0 - k0).astype(F32))

    n_steps = (cb + MOBA_STEP_BLOCKS - 1) // MOBA_STEP_BLOCKS
    carry = lax.fori_loop(0, n_steps, past_step, _flash_init(nq))
    kc0 = pl.multiple_of(q0, MOBA_BLOCK)
    s = _dot_nt(k_ref[pl.ds(kc0, MOBA_BLOCK), :], q) - bias_ref[0:MOBA_BLOCK, :]
    s = jnp.where(off_ref[0:MOBA_BLOCK, :] <= 0, s, NEG_INF)
    _, l, acc = _online_step_t(carry, [(s, vt_ref[:, pl.ds(kc0, MOBA_BLOCK)])], jnp.zeros((1, 1), F32))
    o_ref[...] = (acc / jnp.where(l > 0, l, 1.0)).T.astype(o_ref.dtype)


def _moba(proj, mv_t, slopes, off, bias, batch, seq, cols):
    n_blocks = seq // MOBA_BLOCK
    top_m = min(MOBA_TOPK, n_blocks)
    return pl.pallas_call(
        functools.partial(_moba_kernel, top_m=top_m),
        out_shape=jax.ShapeDtypeStruct((batch * seq, MOBA_HEADS * HEAD_DIM), MXU_DTYPE),
        grid=(batch, MOBA_HEADS, n_blocks),
        in_specs=[pl.BlockSpec((MOBA_BLOCK, HEAD_DIM), lambda b, h, c: (b * n_blocks + c, cols["mq"] + h)),
                  pl.BlockSpec((seq, HEAD_DIM), lambda b, h, c: (b, cols["mk"] + h)),
                  pl.BlockSpec((HEAD_DIM, seq), lambda b, h, c: (b * cols["vt_heads"] + cols["mv_t"] + h, 0)),
                  pl.BlockSpec((1, 1, LANES), lambda b, h, c: (h, 0, 0)),
                  pl.BlockSpec((MOBA_STEP, MOBA_BLOCK), lambda b, h, c: (0, 0)),
                  pl.BlockSpec((1, MOBA_STEP, MOBA_BLOCK), lambda b, h, c: (h, 0, 0))],
        out_specs=pl.BlockSpec((MOBA_BLOCK, HEAD_DIM), lambda b, h, c: (b * n_blocks + c, h)),
        scratch_shapes=[pltpu.VMEM((LANES, HEAD_DIM), F32), pltpu.VMEM((LANES, MOBA_BLOCK), F32)],
        compiler_params=_cparams("parallel", "parallel", "arbitrary"),
        name="moba_attention",
    )(proj, proj, mv_t, slopes, off, bias)


def _outproj_ln_kernel(a1_ref, a2_ref, x_ref, w1_ref, w2_ref, g_ref, b_ref, o_ref):
    y = DN_ALPHA * x_ref[...] + _dot(a1_ref[...], w1_ref[...]) + _dot(a2_ref[...], w2_ref[...])
    o_ref[...] = _layernorm(y, g_ref[...], b_ref[...])


def _outproj_ln(a1, a2, x, w1, w2, g, b, tm):
    t, d = x.shape
    k1, k2 = a1.shape[1], a2.shape[1]
    const = lambda shape: pl.BlockSpec(shape, lambda i: (0, 0))
    return pl.pallas_call(
        _outproj_ln_kernel,
        out_shape=jax.ShapeDtypeStruct((t, d), F32),
        grid=(t // tm,),
        in_specs=[pl.BlockSpec((tm, k1), lambda i: (i, 0)), pl.BlockSpec((tm, k2), lambda i: (i, 0)),
                  pl.BlockSpec((tm, d), lambda i: (i, 0)),
                  const((k1, d)), const((k2, d)), const((1, d)), const((1, d))],
        out_specs=pl.BlockSpec((tm, d), lambda i: (i, 0)),
        compiler_params=_cparams("parallel"),
        name="out_proj_ln",
    )(a1, a2, x, w1, w2, g, b)


def _xattn_ln_kernel(h_ref, wq_ref, k_ref, v_ref, wo_ref, g_ref, b_ref, o_ref, o_lowp_ref):
    h = h_ref[...]
    q = (_dot(h.astype(MXU_DTYPE), wq_ref[...]) * (XA_DIM ** -0.5)).astype(MXU_DTYPE)
    outs = []
    for hd in range(XA_HEADS):
        cs = slice(hd * XA_DIM, (hd + 1) * XA_DIM)
        s = _dot_nt(q[:, cs], k_ref[:, cs])
        m = jnp.max(s, -1, keepdims=True)
        e = jnp.exp(s - m)
        p = e / jnp.sum(e, -1, keepdims=True)
        outs.append(_dot(p.astype(MXU_DTYPE), v_ref[:, cs]).astype(MXU_DTYPE))
    o = jnp.concatenate(outs, axis=1)
    y = _layernorm(DN_ALPHA * h + _dot(o, wo_ref[...]), g_ref[...], b_ref[...])
    o_ref[...] = y
    o_lowp_ref[...] = y.astype(o_lowp_ref.dtype)


def _xattn_ln(h, kv, wq, wo, g, b, batch, seq, mem_len, tm):
    t, d = h.shape
    e = XA_HEADS * XA_DIM
    nt = seq // tm
    const = lambda shape: pl.BlockSpec(shape, lambda bi, i: (0, 0))
    return pl.pallas_call(
        _xattn_ln_kernel,
        out_shape=(jax.ShapeDtypeStruct((t, d), F32), jax.ShapeDtypeStruct((t, d), MXU_DTYPE)),
        grid=(batch, nt),
        in_specs=[pl.BlockSpec((tm, d), lambda bi, i: (bi * nt + i, 0)),
                  const((d, e)),
                  pl.BlockSpec((mem_len, e), lambda bi, i: (bi, 0)),
                  pl.BlockSpec((mem_len, e), lambda bi, i: (bi, 1)),
                  const((e, d)), const((1, d)), const((1, d))],
        out_specs=(pl.BlockSpec((tm, d), lambda bi, i: (bi * nt + i, 0)),
                   pl.BlockSpec((tm, d), lambda bi, i: (bi * nt + i, 0))),
        compiler_params=_cparams("parallel", "parallel"),
        name="xattn_ln",
    )(h, wq, kv, kv, wo, g, b)


def _peer_route_kernel(h_ref, wq_ref, sk_ref, gw_ref, ei_ref):
    hh = pl.program_id(1)
    tm = h_ref.shape[0]
    k_top = PEER_TOPK
    q = _dot(h_ref[...], wq_ref[...]).astype(MXU_DTYPE)
    key_row = lax.broadcasted_iota(jnp.int32, (PEER_NKEYS, tm), 0).astype(F32)
    tops = []
    for half in range(2):
        s = _dot_nt(sk_ref[0, half], q[:, half * PEER_NKEYS:(half + 1) * PEER_NKEYS])
        tops.append(_top_values_t(s, key_row, k_top))
    (v0, i0), (v1, i1) = tops

    def grid(a, b):
        first = [a[0:1] + b]
        mid = [a[i:i + 1] + b[0:SUBLANES] for i in range(1, SUBLANES)]
        return jnp.concatenate(first + mid + [a[SUBLANES:] + b[0:1]], axis=0)

    cand = grid(v0, v1)
    cidx = grid(i0 * float(PEER_NKEYS), i1)
    n_cand = cand.shape[0]
    cand_row = lax.broadcasted_iota(jnp.int32, (n_cand, tm), 0).astype(F32)
    pick_row = lax.broadcasted_iota(jnp.int32, (k_top, tm), 0)

    def pick_expert(k, carry):
        cd, tv, te = carry
        m = jnp.max(cd, 0, keepdims=True)
        j = jnp.min(jnp.where(cd == m, cand_row, float(n_cand)), 0, keepdims=True)
        hit = cand_row == j
        e = jnp.sum(jnp.where(hit, cidx, 0.0), 0, keepdims=True)
        here = pick_row == k
        return jnp.where(hit, NEG_INF, cd), jnp.where(here, m, tv), jnp.where(here, e, te)

    zeros = jnp.zeros((k_top, tm), F32)
    _, tv, te = lax.fori_loop(0, k_top, pick_expert, (cand, zeros, zeros))
    ex = jnp.exp(tv - jnp.max(tv, 0, keepdims=True))
    rows = pl.ds(pl.multiple_of(hh * k_top, k_top), k_top)
    gw_ref[rows, :] = ex / jnp.sum(ex, 0, keepdims=True)
    ei_ref[rows, :] = te.astype(jnp.int32)


def _peer_route(h, wq, sub_keys, tm):
    t, d = h.shape
    return pl.pallas_call(
        _peer_route_kernel,
        out_shape=(jax.ShapeDtypeStruct((PEER_PICKS, t), F32), jax.ShapeDtypeStruct((PEER_PICKS, t), jnp.int32)),
        grid=(t // tm, PEER_HEADS),
        in_specs=[pl.BlockSpec((tm, d), lambda i, hh: (i, 0)),
                  pl.BlockSpec((d, PEER_QDIM), lambda i, hh: (0, hh)),
                  pl.BlockSpec((1, 2, PEER_NKEYS, PEER_QDIM // 2), lambda i, hh: (hh, 0, 0, 0))],
        out_specs=(pl.BlockSpec((PEER_PICKS, tm), lambda i, hh: (0, i)),
                   pl.BlockSpec((PEER_PICKS, tm), lambda i, hh: (0, i))),
        compiler_params=_cparams("parallel", "arbitrary"),
        name="peer_route",
    )(h, wq, sub_keys)


def _peer_expert_kernel(ei_ref, gw_ref, x_ref, uv_ref, o_ref, *scratch):
    bufs, sem_ref = scratch[:PEER_SLOTS], scratch[PEER_SLOTS]
    n_tok = x_ref.shape[0]
    half = PEER_SLAB // 2
    groups = PEER_PICKS // SUBLANES
    lookahead = PEER_SLOTS - 1

    def slab(slot, j):
        return bufs[slot].at[pl.ds(j * PEER_SLAB_PITCH, PEER_SLAB)]

    def issue(tok, slot):
        for j in range(PEER_PICKS):
            pltpu.make_async_copy(uv_ref.at[ei_ref[j, tok]], slab(slot, j), sem_ref.at[slot]).start(priority=j % 2)

    def wait(slot):
        for j in range(PEER_PICKS):
            pltpu.make_async_copy(uv_ref.at[0], slab(slot, j), sem_ref.at[slot]).wait()

    gw_t = gw_ref[...]
    tok_lane = lax.broadcasted_iota(jnp.int32, gw_t.shape, 1)

    def compute(tok, slot):
        def words(j0, r):
            w = bufs[slot][pl.ds(j0 * PEER_SLAB_PITCH + r, SUBLANES, stride=PEER_SLAB_PITCH), :]
            return (lax.bitcast_convert_type(w << 16, F32),
                    lax.bitcast_convert_type(w & jnp.uint32(0xFFFF0000), F32))

        xs = [x_ref[tok, pl.ds(r, 1), :] for r in range(2 * half)]
        parts = []
        for jg in range(groups):
            acc = None
            for r in range(half):
                lo, hi = words(jg * SUBLANES, r)
                term = lo * xs[r] + hi * xs[half + r]
                acc = term if acc is None else acc + term
            parts.append(jnp.sum(acc, -1, keepdims=True))
        a = jnp.concatenate(parts, axis=0)
        gw_col = jnp.sum(jnp.where(tok_lane == tok, gw_t, 0.0), -1, keepdims=True)
        w = gw_col * jax.nn.gelu(a)
        ws = [w[jg * SUBLANES:(jg + 1) * SUBLANES] for jg in range(groups)]
        for r in range(half):
            acc_lo = acc_hi = None
            for jg in range(groups):
                lo, hi = words(jg * SUBLANES, half + r)
                acc_lo = lo * ws[jg] if acc_lo is None else acc_lo + lo * ws[jg]
                acc_hi = hi * ws[jg] if acc_hi is None else acc_hi + hi * ws[jg]
            o_ref[tok, pl.ds(r, 1), :] = jnp.sum(acc_lo, 0, keepdims=True)
            o_ref[tok, pl.ds(half + r, 1), :] = jnp.sum(acc_hi, 0, keepdims=True)

    def round_of_slots(base, n_issue):
        for s in range(PEER_SLOTS):
            wait(s)
            if s < n_issue:
                issue(base + s + lookahead, (s + lookahead) % PEER_SLOTS)
            compute(base + s, s)

    for s in range(lookahead):
        issue(s, s)
    n_rounds = n_tok // PEER_SLOTS

    def round_body(i, _):
        round_of_slots(i * PEER_SLOTS, PEER_SLOTS)
        return 0

    lax.fori_loop(0, n_rounds - 1, round_body, 0)
    round_of_slots((n_rounds - 1) * PEER_SLOTS, PEER_SLOTS - lookahead)


def _peer_experts(eidx_t, gw_t, x3, uv):
    t = x3.shape[0]
    tt = PEER_TOK_TILE
    return pl.pallas_call(
        _peer_expert_kernel,
        out_shape=jax.ShapeDtypeStruct((t, PEER_SLAB, LANES), F32),
        grid=(t // tt,),
        in_specs=[pl.BlockSpec((PEER_PICKS, tt), lambda i: (0, i), memory_space=pltpu.SMEM),
                  pl.BlockSpec((PEER_PICKS, tt), lambda i: (0, i)),
                  pl.BlockSpec((tt, PEER_SLAB, LANES), lambda i: (i, 0, 0)),
                  pl.BlockSpec(memory_space=pl.ANY)],
        out_specs=pl.BlockSpec((tt, PEER_SLAB, LANES), lambda i: (i, 0, 0)),
        scratch_shapes=[pltpu.VMEM((PEER_PICKS * PEER_SLAB_PITCH, LANES), jnp.uint32) for _ in range(PEER_SLOTS)]
                       + [pltpu.SemaphoreType.DMA((PEER_SLOTS,))],
        compiler_params=_cparams("arbitrary"),
        name="peer_experts",
    )(eidx_t, gw_t, x3, uv)


def _add_ln_kernel(h_ref, f_ref, g_ref, b_ref, o_ref):
    o_ref[...] = _layernorm(DN_ALPHA * h_ref[...] + f_ref[...], g_ref[...], b_ref[...])


def _add_ln(h, f, g, b, tm):
    t, d = h.shape
    row = pl.BlockSpec((tm, d), lambda i: (i, 0))
    const = pl.BlockSpec((1, d), lambda i: (0, 0))
    return pl.pallas_call(
        _add_ln_kernel,
        out_shape=jax.ShapeDtypeStruct((t, d), F32),
        grid=(t // tm,),
        in_specs=[row, row, const, const],
        out_specs=row,
        compiler_params=_cparams("parallel"),
        name="add_ln",
    )(h, f, g, b)


def _alibi_slopes():
    s = (2.0 ** (-8.0 * (np.arange(N_MIX_HEADS) + 1) / N_MIX_HEADS)).astype(np.float32)
    return s[0::2], s[1::2]


def _mixer(x2, batch, seq, w_in, pe_k, w1_k, w2_k, pe_v, w1_v, w2_v):
    hd = HEAD_DIM
    sizes = [NSA_HEADS * hd] + [NSA_KV_GROUPS * hd] * 6 + [NSA_HEADS * 3] + [MOBA_HEADS * hd] * 3
    offs = np.concatenate([[0], np.cumsum(sizes)])
    sec = {n: w_in[:, offs[i]:offs[i + 1]] for i, n in enumerate(
        ["nq", "kc", "vc", "ks", "vs", "kw", "vw", "gl", "mq", "mk", "mv"])}
    order = ["nq", "ks", "kw", "vw", "mq", "mk"]
    cols, at = {}, 0
    for n in order:
        cols[n] = at // LANES
        at += sec[n].shape[1]
    w_main = jnp.concatenate([sec[n] for n in order], axis=1).astype(MXU_DTYPE)
    scale = hd ** -0.5
    col_scale = jnp.concatenate([
        jnp.full((1, sec[n].shape[1]), scale if n in ("nq", "mq") else 1.0, F32) for n in order], axis=1)
    per_group = NSA_HPG * 3
    gl_cols = [jnp.pad(sec["gl"][:, g * per_group:(g + 1) * per_group], ((0, 0), (0, LANES - per_group)))
               for g in range(NSA_KV_GROUPS)]
    w_aux = jnp.concatenate([sec["kc"], sec["vc"]] + gl_cols, axis=1).astype(MXU_DTYPE)
    cols["gl_aux"] = (2 * NSA_KV_GROUPS * hd) // LANES

    xb = x2.astype(MXU_DTYPE)
    tm = 512 if x2.shape[0] % 512 == 0 else 256
    proj = _matmul(xb, w_main, col_scale, MXU_DTYPE, tm, 768, "in_proj")
    w_vt = jnp.concatenate([sec["vs"], sec["mv"]], axis=1).T.astype(MXU_DTYPE)
    v_t = _project_t(w_vt, xb, batch, seq, tm)
    cols["vs_t"], cols["mv_t"], cols["vt_heads"] = 0, NSA_KV_GROUPS, NSA_KV_GROUPS + MOBA_HEADS
    aux = _matmul(xb, w_aux, jnp.ones((1, w_aux.shape[1]), F32), F32, tm, 256, "in_proj_aux")

    ns = seq // NSA_CMP_STRIDE

    def strips(col0):
        raw = aux[:, col0:col0 + NSA_KV_GROUPS * hd].reshape(batch, seq, NSA_KV_GROUPS, hd)
        return raw.transpose(0, 2, 1, 3).reshape(batch * NSA_KV_GROUPS, ns, NSA_CMP_STRIDE * hd)

    kc = _compress(strips(0), pe_k, w1_k, w2_k)
    vc = _compress(strips(NSA_KV_GROUPS * hd), pe_v, w1_v, w2_v)

    slope_n, slope_m = _alibi_slopes()
    sn = np.repeat(slope_n.reshape(NSA_KV_GROUPS, NSA_HPG), Q_BLK, axis=1)
    sn_rows = jnp.asarray(np.broadcast_to(sn[:, :, None], sn.shape + (LANES,)).copy())
    sn_lanes = jnp.asarray(sn[:, None, :].copy())
    sm = jnp.asarray(np.broadcast_to(slope_m[:, None, None], (MOBA_HEADS, 1, LANES)).copy())

    key = lambda n: lax.broadcasted_iota(jnp.int32, (n, 1), 0)
    off_n = key(SEL_STEP) - (lax.broadcasted_iota(jnp.int32, (1, NSA_ROWS), 1) & (Q_BLK - 1))
    bias_n = sn_lanes * (-off_n).astype(F32)[None]
    off_m = key(MOBA_STEP) - lax.broadcasted_iota(jnp.int32, (1, MOBA_BLOCK), 1)
    bias_m = jnp.asarray(slope_m)[:, None, None] * (-off_m).astype(F32)[None]

    o_nsa = _nsa(proj, aux, kc, vc, v_t, sn_rows, sn_lanes, off_n, bias_n, batch, seq, cols)
    o_moba = _moba(proj, v_t, sm, off_m, bias_m, batch, seq, cols)
    return o_nsa, o_moba


def _memory_xattn_ln(h, mem2, batch, seq, wq, wkv, wo, g, b):
    mem_len = mem2.shape[0] // batch
    e2 = wkv.shape[1]
    kv = _matmul(mem2.astype(MXU_DTYPE), wkv.astype(MXU_DTYPE), jnp.ones((1, e2), F32), MXU_DTYPE,
                 mem_len, e2 // 2, "xattn_kv")
    return _xattn_ln(h, kv, wq.astype(MXU_DTYPE), wo.astype(MXU_DTYPE), g, b, batch, seq, mem_len, 256)


def _pack_tables_kernel(u_ref, v_ref, o_ref):
    def words(x):
        bits = lax.bitcast_convert_type(x.astype(jnp.bfloat16).astype(F32), jnp.uint32)
        half = x.shape[1] // 2
        return (bits[:, :half] >> 16) | (bits[:, half:] & jnp.uint32(0xFFFF0000))

    n = u_ref.shape[0]
    per_table = PEER_SLAB // 2
    for t, ref in enumerate((u_ref, v_ref)):
        w = words(ref[...])
        for r in range(per_table):
            o_ref[pl.ds(t * per_table + r, n, stride=PEER_SLAB), :] = w[:, r * LANES:(r + 1) * LANES]


def _pack_expert_tables(exp_u, exp_v):
    n_exp, d = exp_u.shape
    blk = 256
    packed = pl.pallas_call(
        _pack_tables_kernel,
        out_shape=jax.ShapeDtypeStruct((n_exp * PEER_SLAB, LANES), jnp.uint32),
        grid=(n_exp // blk,),
        in_specs=[pl.BlockSpec((blk, d), lambda i: (i, 0)), pl.BlockSpec((blk, d), lambda i: (i, 0))],
        out_specs=pl.BlockSpec((blk * PEER_SLAB, LANES), lambda i: (i, 0)),
        compiler_params=_cparams("parallel"),
        name="peer_pack_tables",
    )(exp_u, exp_v)
    return packed.reshape(n_exp, PEER_SLAB, LANES)


def _peer_ln(h, h_lowp, wq, sub_keys, exp_u, exp_v, g, b):
    t, d = h.shape
    gw_t, eidx_t = _peer_route(h_lowp, wq.astype(MXU_DTYPE), sub_keys.astype(MXU_DTYPE), 256)
    uv = _pack_expert_tables(exp_u, exp_v)
    f = _peer_experts(eidx_t, gw_t, h.reshape(t, PEER_SLAB, LANES), uv).reshape(t, d)
    return _add_ln(h, f, g, b, 256)


def kernel(x, mem, w_in, cmp_pe_k, cmp_w1_k, cmp_w2_k, cmp_pe_v, cmp_w1_v, cmp_w2_v, w_out, ln1_g, ln1_b,
           xa_wq, xa_wkv, xa_wo, ln2_g, ln2_b, peer_wq, peer_subkeys, peer_u, peer_v, ln3_g, ln3_b):
    batch, seq, d = x.shape
    assert seq % MOBA_STEP == 0 and WIN_SPAN <= seq <= NSA_SEL_BLOCK * LANES
    assert d == PEER_SLAB * LANES and w_in.shape[0] == DEPTH and PEER_TOPK == 2 * SUBLANES
    row = lambda v: v.reshape(1, d)
    h = x.reshape(batch * seq, d)
    mem2 = mem.reshape(-1, d)
    for l in range(DEPTH):
        o_nsa, o_moba = _mixer(h, batch, seq, w_in[l], cmp_pe_k[l], cmp_w1_k[l], cmp_w2_k[l],
                               cmp_pe_v[l], cmp_w1_v[l], cmp_w2_v[l])
        wo = w_out[l].astype(MXU_DTYPE)
        k1 = o_nsa.shape[1]
        h = _outproj_ln(o_nsa, o_moba, h, wo[:k1], wo[k1:], row(ln1_g[l]), row(ln1_b[l]), 256)
        h, h_lowp = _memory_xattn_ln(h, mem2, batch, seq, xa_wq[l], xa_wkv[l], xa_wo[l],
                                     row(ln2_g[l]), row(ln2_b[l]))
        h = _peer_ln(h, h_lowp, peer_wq[l], peer_subkeys[l], peer_u[l], peer_v[l], row(ln3_g[l]), row(ln3_b[l]))
    return h.reshape(batch, seq, d)
```

```python
import functools

import numpy as np
import jax
import jax.numpy as jnp
from jax import lax
from jax.experimental import pallas as pl
from jax.experimental.pallas import tpu as pltpu

F32 = jnp.float32
MXU_DTYPE = jnp.bfloat16
NEG_INF = float("-inf")

LANES = 128
SUBLANES = 8
VMEM_LIMIT = 48 * 1024 * 1024

HEAD_DIM = 128
N_MIX_HEADS = 16
NSA_HEADS = 8
NSA_KV_GROUPS = 2
NSA_HPG = NSA_HEADS // NSA_KV_GROUPS
NSA_CMP_STRIDE = 16
NSA_CMP_LEN = 32
NSA_SEL_BLOCK = 64
NSA_TOPN = 16
NSA_WINDOW = 512
NSA_FORCE_BONUS = 1.0e4
MOBA_HEADS = 8
MOBA_BLOCK = 256
MOBA_TOPK = 3
Q_BLK = 64
XA_HEADS = 4
XA_DIM = 128
PEER_HEADS = 8
PEER_NKEYS = 128
PEER_QDIM = 256
PEER_TOPK = 16
DEPTH = 1
DN_ALPHA = (2 * DEPTH) ** 0.25
LN_EPS = 1e-5

NSA_ROWS = NSA_HPG * Q_BLK
WIN_SPAN = NSA_WINDOW + 2 * Q_BLK
SEL_STEP_BLOCKS = 2 * SUBLANES
SEL_STEP = SEL_STEP_BLOCKS * NSA_SEL_BLOCK
MOBA_STEP_BLOCKS = 4
MOBA_STEP = MOBA_STEP_BLOCKS * MOBA_BLOCK
PEER_PICKS = PEER_HEADS * PEER_TOPK
PEER_SLAB = 16
PEER_SLAB_PITCH = 24
PEER_SLOTS = 16
PEER_TOK_TILE = 2 * LANES


def _cparams(*sem):
    return pltpu.CompilerParams(dimension_semantics=sem, vmem_limit_bytes=VMEM_LIMIT)


def _dot(a, b):
    return jnp.dot(a, b, preferred_element_type=F32)


def _dot_nt(a, b):
    return lax.dot_general(a, b, (((1,), (1,)), ((), ())), preferred_element_type=F32)


def _split3(x):
    hi = x.astype(MXU_DTYPE)
    r1 = x - hi.astype(F32)
    mid = r1.astype(MXU_DTYPE)
    lo = (r1 - mid.astype(F32)).astype(MXU_DTYPE)
    return hi, mid, lo


def _masked_softmax(s, mask):
    s = jnp.where(mask, s, NEG_INF)
    m = jnp.max(s, -1, keepdims=True)
    m = jnp.where(jnp.isfinite(m), m, 0.0)
    e = jnp.where(mask, jnp.exp(s - m), 0.0)
    d = jnp.sum(e, -1, keepdims=True)
    return e / jnp.where(d > 0, d, 1.0)


def _flash_partial(s, row_shift, vt):
    m = jnp.max(s, 0, keepdims=True)
    p = jnp.exp(s - jnp.where(m == NEG_INF, 0.0, m))
    return m - row_shift, jnp.sum(p, 0, keepdims=True), _dot(vt, p.astype(MXU_DTYPE))


def _online_step_t(carry, pieces, row_shift):
    m_i, l_i, acc = carry
    parts = [_flash_partial(s, row_shift, vt) for s, vt in pieces]
    m_new = m_i
    for m_p, _, _ in parts:
        m_new = jnp.maximum(m_new, m_p)
    m_safe = jnp.where(m_new == NEG_INF, 0.0, m_new)
    alpha = jnp.exp(m_i - m_safe)
    l_new, acc_new = alpha * l_i, alpha * acc
    for m_p, l_p, acc_p in parts:
        alpha = jnp.exp(m_p - m_safe)
        l_new, acc_new = l_new + alpha * l_p, acc_new + alpha * acc_p
    return m_new, l_new, acc_new


def _flash_init(n_q):
    return (jnp.full((1, n_q), NEG_INF, F32), jnp.zeros((1, n_q), F32), jnp.zeros((HEAD_DIM, n_q), F32))


def _layernorm(y, g, b):
    mu = jnp.mean(y, -1, keepdims=True)
    var = jnp.mean(jnp.square(y - mu), -1, keepdims=True)
    return (y - mu) * lax.rsqrt(var + LN_EPS) * g + b


def _top_select_t(score, row, n_pick):
    sentinel = float(score.shape[0])

    def body(_, carry):
        s, sel = carry
        m = jnp.max(s, 0, keepdims=True)
        idx = jnp.min(jnp.where(s == m, row, sentinel), 0, keepdims=True)
        hit = row == idx
        return jnp.where(hit, NEG_INF, s), jnp.where(hit, 1.0, sel)

    _, sel = lax.fori_loop(0, n_pick, body, (score, jnp.zeros_like(score)))
    return sel


def _top_values_t(scores, row, n_pick):
    sentinel = float(scores[0].shape[0])
    n = scores[0].shape[1]
    out_row = lax.broadcasted_iota(jnp.int32, (n_pick, n), 0)

    def body(k, carry):
        here = out_row == k
        out = []
        for s, vals, idxs in carry:
            m = jnp.max(s, 0, keepdims=True)
            idx = jnp.min(jnp.where(s == m, row, sentinel), 0, keepdims=True)
            out.append((jnp.where(row == idx, NEG_INF, s), jnp.where(here, m, vals), jnp.where(here, idx, idxs)))
        return tuple(out)

    zeros = jnp.zeros((n_pick, n), F32)
    done = lax.fori_loop(0, n_pick, body, tuple((s, zeros, zeros) for s in scores))
    return [(vals, idxs) for _, vals, idxs in done]


def _mm_kernel(a_ref, b_ref, s_ref, o_ref):
    acc = _dot(a_ref[...], b_ref[...])
    o_ref[...] = (acc * s_ref[...]).astype(o_ref.dtype)


def _matmul(a, b, col_scale, out_dtype, tm, tn, name):
    m, k = a.shape
    n = b.shape[1]
    return pl.pallas_call(
        _mm_kernel,
        out_shape=jax.ShapeDtypeStruct((m, n), out_dtype),
        grid=(m // tm, n // tn),
        in_specs=[pl.BlockSpec((tm, k), lambda i, j: (i, 0)),
                  pl.BlockSpec((k, tn), lambda i, j: (0, j)),
                  pl.BlockSpec((1, tn), lambda i, j: (0, j))],
        out_specs=pl.BlockSpec((tm, tn), lambda i, j: (i, j)),
        compiler_params=_cparams("parallel", "parallel"),
        name=name,
    )(a, b, col_scale)


def _proj_t_kernel(w_ref, x_ref, o_ref):
    o_ref[...] = _dot_nt(w_ref[...], x_ref[...]).astype(o_ref.dtype)


def _project_t(w_t, xb, batch, seq, tm):
    n, k = w_t.shape
    per = seq // tm
    return pl.pallas_call(
        _proj_t_kernel,
        out_shape=jax.ShapeDtypeStruct((batch * n, seq), MXU_DTYPE),
        grid=(batch * per,),
        in_specs=[pl.BlockSpec((n, k), lambda i: (0, 0)), pl.BlockSpec((tm, k), lambda i: (i, 0))],
        out_specs=pl.BlockSpec((n, tm), lambda i: (i // per, i % per)),
        compiler_params=_cparams("parallel"),
        name="in_proj_vt",
    )(w_t, xb)


def _compress_kernel(r_ref, pelo_ref, pehi_ref, w1lo_ref, w1hi_ref, w2_ref, o_ref):
    r = r_ref[0]
    ns = r.shape[0]
    lo = _dot((r + pelo_ref[...]).astype(MXU_DTYPE), w1lo_ref[...])
    hi = _dot((r + pehi_ref[...]).astype(MXU_DTYPE), w1hi_ref[...])
    hid = jax.nn.gelu(lo + pltpu.roll(hi, ns - 1, 0))
    o_ref[0] = _dot(hid.astype(MXU_DTYPE), w2_ref[...]).astype(o_ref.dtype)


def _compress(strips, pe, w1, w2):
    bg, ns, width = strips.shape
    half = NSA_CMP_LEN // 2
    pelo = pe[:half].reshape(1, width)
    pehi = pe[half:].reshape(1, width)
    w1lo = w1[:half].reshape(width, HEAD_DIM).astype(MXU_DTYPE)
    w1hi = w1[half:].reshape(width, HEAD_DIM).astype(MXU_DTYPE)
    const = lambda shape: pl.BlockSpec(shape, lambda i: (0,) * len(shape))
    return pl.pallas_call(
        _compress_kernel,
        out_shape=jax.ShapeDtypeStruct((bg, ns, HEAD_DIM), MXU_DTYPE),
        grid=(bg,),
        in_specs=[pl.BlockSpec((1, ns, width), lambda i: (i, 0, 0)),
                  const((1, width)), const((1, width)),
                  const((width, HEAD_DIM)), const((width, HEAD_DIM)), const((HEAD_DIM, HEAD_DIM))],
        out_specs=pl.BlockSpec((1, ns, HEAD_DIM), lambda i: (i, 0, 0)),
        compiler_params=_cparams("parallel"),
        name="nsa_compress",
    )(strips, pelo, pehi, w1lo, w1hi, w2.astype(MXU_DTYPE))


def _nsa_kernel(q_ref, kc_ref, vc_ref, ks_ref, vst_ref, kw_ref, vw_ref, gl_ref, slope_ref, slope_t_ref,
                off_ref, bias_ref, o_ref, selt_ref, flag_ref, *, top_n):
    c = pl.program_id(2)
    q0 = c * Q_BLK
    rows = NSA_ROWS
    q = q_ref[...]
    qh = jnp.concatenate([q[:, h * HEAD_DIM:(h + 1) * HEAD_DIM] for h in range(NSA_HPG)], axis=0)
    slope = slope_ref[0][:, 0:1]
    slope_t = slope_t_ref[0]

    def tpos(width):
        return q0 + (lax.broadcasted_iota(jnp.int32, (rows, width), 0) & (Q_BLK - 1))

    kc = kc_ref[0]
    ns = kc.shape[0]
    cend = lax.broadcasted_iota(jnp.int32, (rows, ns), 1) * NSA_CMP_STRIDE + (NSA_CMP_LEN - 1)
    t_c = tpos(ns)
    s_c = _dot_nt(qh, kc) - slope * (t_c - cend).astype(F32)
    p_c = _masked_softmax(s_c, cend <= t_c)
    o_c = _dot(p_c.astype(MXU_DTYPE), vc_ref[0])

    p_sum = p_c[0:Q_BLK]
    for h in range(1, NSA_HPG):
        p_sum = p_sum + p_c[h * Q_BLK:(h + 1) * Q_BLK]
    p_two = jnp.concatenate([p_sum, p_sum], axis=0)
    ratio = NSA_SEL_BLOCK // NSA_CMP_STRIDE
    gj = lax.broadcasted_iota(jnp.int32, (LANES, ns), 0) * ratio
    gi = lax.broadcasted_iota(jnp.int32, (LANES, ns), 1)
    gather01 = jnp.where((gi >= gj - 1) & (gi <= gj + ratio - 1), 1.0, 0.0).astype(MXU_DTYPE)
    hi, mid, lo = _split3(p_two)
    imp_t = _dot_nt(gather01, hi) + _dot_nt(gather01, mid) + _dot_nt(gather01, lo)
    jrow_i = lax.broadcasted_iota(jnp.int32, (LANES, LANES), 0)
    forced = (jrow_i == 0) | (jrow_i == c) | (jrow_i == c - 1)
    valid = jrow_i <= c
    score_t = jnp.where(valid, imp_t + jnp.where(forced, NSA_FORCE_BONUS, 0.0), NEG_INF)
    sel_t = jnp.where(valid, _top_select_t(score_t, jrow_i.astype(F32), top_n), 0.0)
    selt_ref[...] = jnp.concatenate([sel_t, sel_t], axis=1)
    picked_rows = jnp.max(sel_t, 1, keepdims=True)
    for kb in range(LANES // SEL_STEP_BLOCKS):
        step_rows = picked_rows[kb * SEL_STEP_BLOCKS:(kb + 1) * SEL_STEP_BLOCKS]
        flag_ref[kb] = (jnp.max(step_rows, 0, keepdims=True)[0, 0] > 0.5).astype(jnp.int32)

    bias_ref = bias_ref.at[0]

    def step_picks(kb):
        return selt_ref[pl.ds(pl.multiple_of(kb * SEL_STEP_BLOCKS, SEL_STEP_BLOCKS), SEL_STEP_BLOCKS), :]

    def sel_step(kb, carry, causal):
        k0 = pl.multiple_of(kb * SEL_STEP, SEL_STEP)
        picks = step_picks(kb)
        n_half = SEL_STEP // 2
        halves = []
        for hf in range(2):
            kh = pl.multiple_of(k0 + hf * n_half, n_half)
            s = _dot_nt(ks_ref[pl.ds(kh, n_half), :], qh) - bias_ref[hf * n_half:(hf + 1) * n_half, :]
            masked = []
            for i in range(SEL_STEP_BLOCKS // 2):
                r = slice(i * NSA_SEL_BLOCK, (i + 1) * NSA_SEL_BLOCK)
                b = hf * (SEL_STEP_BLOCKS // 2) + i
                ok = picks[b:b + 1, :] > 0.5
                if causal:
                    ok = ok & (off_ref[b * NSA_SEL_BLOCK:(b + 1) * NSA_SEL_BLOCK, :] <= q0 - k0)
                masked.append(jnp.where(ok, s[r], NEG_INF))
            halves.append((jnp.concatenate(masked, axis=0), vst_ref[:, pl.ds(kh, n_half)]))
        return _online_step_t(carry, halves, slope_t * (q0 - k0).astype(F32))

    def maybe_step(kb, carry):
        return lax.cond(flag_ref[kb] > 0, lambda cr: sel_step(kb, cr, causal=False), lambda cr: cr, carry)

    last = c // SEL_STEP_BLOCKS
    carry = lax.fori_loop(0, last, maybe_step, _flash_init(rows))
    _, l_s, acc_s = sel_step(last, carry, causal=True)
    o_s = (acc_s / jnp.where(l_s > 0, l_s, 1.0)).T

    w0 = pl.multiple_of(jnp.maximum(q0 - (WIN_SPAN - Q_BLK), 0), Q_BLK)
    t_w = tpos(WIN_SPAN)
    dist_w = t_w - (w0 + lax.broadcasted_iota(jnp.int32, (rows, WIN_SPAN), 1))
    s_w = _dot_nt(qh, kw_ref[pl.ds(w0, WIN_SPAN), :]) - slope * dist_w.astype(F32)
    p_w = _masked_softmax(s_w, (dist_w >= 0) & (dist_w < NSA_WINDOW))
    o_w = _dot(p_w.astype(MXU_DTYPE), vw_ref[pl.ds(w0, WIN_SPAN), :])

    gates = 1.0 / (1.0 + jnp.exp(-gl_ref[...]))
    for h in range(NSA_HPG):
        r = slice(h * Q_BLK, (h + 1) * Q_BLK)
        o = (gates[:, 3 * h:3 * h + 1] * o_c[r] + gates[:, 3 * h + 1:3 * h + 2] * o_s[r]
             + gates[:, 3 * h + 2:3 * h + 3] * o_w[r])
        o_ref[:, h * HEAD_DIM:(h + 1) * HEAD_DIM] = o.astype(o_ref.dtype)


def _nsa(proj, aux, kc, vc, vs_t, slopes, slopes_t, off, bias, batch, seq, cols):
    n_chunks = seq // Q_BLK
    ns = kc.shape[1]
    gw = NSA_HPG * HEAD_DIM
    top_n = min(NSA_TOPN, seq // NSA_SEL_BLOCK)
    kv = lambda base: pl.BlockSpec((seq, HEAD_DIM), lambda b, g, c: (b, base + g))
    cmp_spec = pl.BlockSpec((1, ns, HEAD_DIM), lambda b, g, c: (b * NSA_KV_GROUPS + g, 0, 0))
    return pl.pallas_call(
        functools.partial(_nsa_kernel, top_n=top_n),
        out_shape=jax.ShapeDtypeStruct((batch * seq, NSA_HEADS * HEAD_DIM), MXU_DTYPE),
        grid=(batch, NSA_KV_GROUPS, n_chunks),
        in_specs=[pl.BlockSpec((Q_BLK, gw), lambda b, g, c: (b * n_chunks + c, g)),
                  cmp_spec, cmp_spec,
                  kv(cols["ks"]),
                  pl.BlockSpec((HEAD_DIM, seq), lambda b, g, c: (b * cols["vt_heads"] + cols["vs_t"] + g, 0)),
                  kv(cols["kw"]), kv(cols["vw"]),
                  pl.BlockSpec((Q_BLK, LANES), lambda b, g, c: (b * n_chunks + c, cols["gl_aux"] + g)),
                  pl.BlockSpec((1, NSA_ROWS, LANES), lambda b, g, c: (g, 0, 0)),
                  pl.BlockSpec((1, 1, NSA_ROWS), lambda b, g, c: (g, 0, 0)),
                  pl.BlockSpec((SEL_STEP, NSA_ROWS), lambda b, g, c: (0, 0)),
                  pl.BlockSpec((1, SEL_STEP, NSA_ROWS), lambda b, g, c: (g, 0, 0))],
        out_specs=pl.BlockSpec((Q_BLK, gw), lambda b, g, c: (b * n_chunks + c, g)),
        scratch_shapes=[pltpu.VMEM((LANES, NSA_ROWS), F32), pltpu.SMEM((LANES // SEL_STEP_BLOCKS,), jnp.int32)],
        compiler_params=_cparams("parallel", "parallel", "arbitrary"),
        name="nsa_attention",
    )(proj, kc, vc, proj, vs_t, proj, proj, aux, slopes, slopes_t, off, bias)


def _moba_kernel(q_ref, k_ref, vt_ref, slope_ref, off_ref, bias_ref, o_ref, kmean_ref, selt_ref, *, top_m):
    cb = pl.program_id(2)
    seq = k_ref.shape[0]
    n_blocks = seq // MOBA_BLOCK
    nq = MOBA_BLOCK

    @pl.when(cb == 0)
    def _():
        kmean_ref[...] = jnp.zeros_like(kmean_ref)
        kf = k_ref[...].astype(F32).reshape(n_blocks, MOBA_BLOCK, HEAD_DIM)
        kmean_ref[0:n_blocks, :] = jnp.sum(kf, axis=1) * (1.0 / MOBA_BLOCK)

    q = q_ref[...]
    slope = slope_ref[0][:, 0:1]
    brow_i = lax.broadcasted_iota(jnp.int32, (LANES, nq), 0)
    gate_t = jnp.where(brow_i < cb, _dot_nt(kmean_ref[...].astype(MXU_DTYPE), q), NEG_INF)
    selt_ref[...] = jnp.where(brow_i < cb, _top_select_t(gate_t, brow_i.astype(F32), top_m), 0.0)

    bias_ref = bias_ref.at[0]
    q0 = cb * MOBA_BLOCK

    def past_step(st, carry):
        k0 = pl.multiple_of(st * MOBA_STEP, MOBA_STEP)
        pair = selt_ref[pl.ds(pl.multiple_of((st // 2) * SUBLANES, SUBLANES), SUBLANES), :]
        picks = jnp.where(st % 2 == 0, pair[0:MOBA_STEP_BLOCKS], pair[MOBA_STEP_BLOCKS:2 * MOBA_STEP_BLOCKS])
        n_half = MOBA_STEP // 2
        halves = []
        for hf in range(2):
            kh = pl.multiple_of(k0 + hf * n_half, n_half)
            s = _dot_nt(k_ref[pl.ds(kh, n_half), :], q) - bias_ref[hf * n_half:(hf + 1) * n_half, :]
            masked = []
            for i in range(MOBA_STEP_BLOCKS // 2):
                b = hf * (MOBA_STEP_BLOCKS // 2) + i
                masked.append(jnp.where(picks[b:b + 1, :] > 0.5, s[i * MOBA_BLOCK:(i + 1) * MOBA_BLOCK], NEG_INF))
            halves.append((jnp.concatenate(masked, axis=0), vt_ref[:, pl.ds(kh, n_half)]))
        return _online_step_t(carry, halves, slope * (q0 - k0).astype(F32))

    n_steps = (cb + MOBA_STEP_BLOCKS - 1) // MOBA_STEP_BLOCKS
    carry = lax.fori_loop(0, n_steps, past_step, _flash_init(nq))
    kc0 = pl.multiple_of(q0, MOBA_BLOCK)
    s = _dot_nt(k_ref[pl.ds(kc0, MOBA_BLOCK), :], q) - bias_ref[0:MOBA_BLOCK, :]
    s = jnp.where(off_ref[0:MOBA_BLOCK, :] <= 0, s, NEG_INF)
    _, l, acc = _online_step_t(carry, [(s, vt_ref[:, pl.ds(kc0, MOBA_BLOCK)])], jnp.zeros((1, 1), F32))
    o_ref[...] = (acc / jnp.where(l > 0, l, 1.0)).T.astype(o_ref.dtype)


def _moba(proj, mv_t, slopes, off, bias, batch, seq, cols):
    n_blocks = seq // MOBA_BLOCK
    top_m = min(MOBA_TOPK, n_blocks)
    return pl.pallas_call(
        functools.partial(_moba_kernel, top_m=top_m),
        out_shape=jax.ShapeDtypeStruct((batch * seq, MOBA_HEADS * HEAD_DIM), MXU_DTYPE),
        grid=(batch, MOBA_HEADS, n_blocks),
        in_specs=[pl.BlockSpec((MOBA_BLOCK, HEAD_DIM), lambda b, h, c: (b * n_blocks + c, cols["mq"] + h)),
                  pl.BlockSpec((seq, HEAD_DIM), lambda b, h, c: (b, cols["mk"] + h)),
                  pl.BlockSpec((HEAD_DIM, seq), lambda b, h, c: (b * cols["vt_heads"] + cols["mv_t"] + h, 0)),
                  pl.BlockSpec((1, 1, LANES), lambda b, h, c: (h, 0, 0)),
                  pl.BlockSpec((MOBA_STEP, MOBA_BLOCK), lambda b, h, c: (0, 0)),
                  pl.BlockSpec((1, MOBA_STEP, MOBA_BLOCK), lambda b, h, c: (h, 0, 0))],
        out_specs=pl.BlockSpec((MOBA_BLOCK, HEAD_DIM), lambda b, h, c: (b * n_blocks + c, h)),
        scratch_shapes=[pltpu.VMEM((LANES, HEAD_DIM), F32), pltpu.VMEM((LANES, MOBA_BLOCK), F32)],
        compiler_params=_cparams("parallel", "parallel", "arbitrary"),
        name="moba_attention",
    )(proj, proj, mv_t, slopes, off, bias)


def _outproj_ln_kernel(a1_ref, a2_ref, x_ref, w1_ref, w2_ref, g_ref, b_ref, o_ref):
    y = DN_ALPHA * x_ref[...] + _dot(a1_ref[...], w1_ref[...]) + _dot(a2_ref[...], w2_ref[...])
    o_ref[...] = _layernorm(y, g_ref[...], b_ref[...])


def _outproj_ln(a1, a2, x, w1, w2, g, b, tm):
    t, d = x.shape
    k1, k2 = a1.shape[1], a2.shape[1]
    const = lambda shape: pl.BlockSpec(shape, lambda i: (0, 0))
    return pl.pallas_call(
        _outproj_ln_kernel,
        out_shape=jax.ShapeDtypeStruct((t, d), F32),
        grid=(t // tm,),
        in_specs=[pl.BlockSpec((tm, k1), lambda i: (i, 0)), pl.BlockSpec((tm, k2), lambda i: (i, 0)),
                  pl.BlockSpec((tm, d), lambda i: (i, 0)),
                  const((k1, d)), const((k2, d)), const((1, d)), const((1, d))],
        out_specs=pl.BlockSpec((tm, d), lambda i: (i, 0)),
        compiler_params=_cparams("parallel"),
        name="out_proj_ln",
    )(a1, a2, x, w1, w2, g, b)


def _xattn_ln_kernel(h_ref, wq_ref, k_ref, v_ref, wo_ref, g_ref, b_ref, o_ref, o_lowp_ref):
    h = h_ref[...]
    q = (_dot(h.astype(MXU_DTYPE), wq_ref[...]) * (XA_DIM ** -0.5)).astype(MXU_DTYPE)
    outs = []
    for hd in range(XA_HEADS):
        cs = slice(hd * XA_DIM, (hd + 1) * XA_DIM)
        s = _dot_nt(q[:, cs], k_ref[:, cs])
        m = jnp.max(s, -1, keepdims=True)
        e = jnp.exp(s - m)
        p = e / jnp.sum(e, -1, keepdims=True)
        outs.append(_dot(p.astype(MXU_DTYPE), v_ref[:, cs]).astype(MXU_DTYPE))
    o = jnp.concatenate(outs, axis=1)
    y = _layernorm(DN_ALPHA * h + _dot(o, wo_ref[...]), g_ref[...], b_ref[...])
    o_ref[...] = y
    o_lowp_ref[...] = y.astype(o_lowp_ref.dtype)


def _xattn_ln(h, kv, wq, wo, g, b, batch, seq, mem_len, tm):
    t, d = h.shape
    e = XA_HEADS * XA_DIM
    nt = seq // tm
    const = lambda shape: pl.BlockSpec(shape, lambda bi, i: (0, 0))
    return pl.pallas_call(
        _xattn_ln_kernel,
        out_shape=(jax.ShapeDtypeStruct((t, d), F32), jax.ShapeDtypeStruct((t, d), MXU_DTYPE)),
        grid=(batch, nt),
        in_specs=[pl.BlockSpec((tm, d), lambda bi, i: (bi * nt + i, 0)),
                  const((d, e)),
                  pl.BlockSpec((mem_len, e), lambda bi, i: (bi, 0)),
                  pl.BlockSpec((mem_len, e), lambda bi, i: (bi, 1)),
                  const((e, d)), const((1, d)), const((1, d))],
        out_specs=(pl.BlockSpec((tm, d), lambda bi, i: (bi * nt + i, 0)),
                   pl.BlockSpec((tm, d), lambda bi, i: (bi * nt + i, 0))),
        compiler_params=_cparams("parallel", "parallel"),
        name="xattn_ln",
    )(h, wq, kv, kv, wo, g, b)


def _peer_route_kernel(h_ref, wq_ref, sk_ref, gw_ref, ei_ref):
    hh = pl.program_id(1)
    tm = h_ref.shape[0]
    k_top = PEER_TOPK
    q = _dot(h_ref[...], wq_ref[...]).astype(MXU_DTYPE)
    key_row = lax.broadcasted_iota(jnp.int32, (PEER_NKEYS, tm), 0).astype(F32)
    scores = [_dot_nt(sk_ref[0, half], q[:, half * PEER_NKEYS:(half + 1) * PEER_NKEYS])
              for half in range(2)]
    (v0, i0), (v1, i1) = _top_values_t(scores, key_row, k_top)

    def grid(a, b):
        first = [a[0:1] + b]
        mid = [a[i:i + 1] + b[0:SUBLANES] for i in range(1, SUBLANES)]
        return jnp.concatenate(first + mid + [a[SUBLANES:] + b[0:1]], axis=0)

    cand = grid(v0, v1)
    cidx = grid(i0 * float(PEER_NKEYS), i1)
    n_cand = cand.shape[0]
    cand_row = lax.broadcasted_iota(jnp.int32, (n_cand, tm), 0).astype(F32)
    pick_row = lax.broadcasted_iota(jnp.int32, (k_top, tm), 0)

    def pick_expert(k, carry):
        cd, tv, te = carry
        m = jnp.max(cd, 0, keepdims=True)
        j = jnp.min(jnp.where(cd == m, cand_row, float(n_cand)), 0, keepdims=True)
        hit = cand_row == j
        e = jnp.sum(jnp.where(hit, cidx, 0.0), 0, keepdims=True)
        here = pick_row == k
        return jnp.where(hit, NEG_INF, cd), jnp.where(here, m, tv), jnp.where(here, e, te)

    zeros = jnp.zeros((k_top, tm), F32)
    _, tv, te = lax.fori_loop(0, k_top, pick_expert, (cand, zeros, zeros))
    ex = jnp.exp(tv - jnp.max(tv, 0, keepdims=True))
    rows = pl.ds(pl.multiple_of(hh * k_top, k_top), k_top)
    gw_ref[rows, :] = ex / jnp.sum(ex, 0, keepdims=True)
    ei_ref[rows, :] = te.astype(jnp.int32)


def _peer_route(h, wq, sub_keys, tm):
    t, d = h.shape
    return pl.pallas_call(
        _peer_route_kernel,
        out_shape=(jax.ShapeDtypeStruct((PEER_PICKS, t), F32), jax.ShapeDtypeStruct((PEER_PICKS, t), jnp.int32)),
        grid=(t // tm, PEER_HEADS),
        in_specs=[pl.BlockSpec((tm, d), lambda i, hh: (i, 0)),
                  pl.BlockSpec((d, PEER_QDIM), lambda i, hh: (0, hh)),
                  pl.BlockSpec((1, 2, PEER_NKEYS, PEER_QDIM // 2), lambda i, hh: (hh, 0, 0, 0))],
        out_specs=(pl.BlockSpec((PEER_PICKS, tm), lambda i, hh: (0, i)),
                   pl.BlockSpec((PEER_PICKS, tm), lambda i, hh: (0, i))),
        compiler_params=_cparams("parallel", "arbitrary"),
        name="peer_route",
    )(h, wq, sub_keys)


def _peer_expert_kernel(ei_ref, gw_ref, x_ref, uv_ref, o_ref, *scratch):
    bufs, sem_ref = scratch[:PEER_SLOTS], scratch[PEER_SLOTS]
    n_tok = x_ref.shape[0]
    half = PEER_SLAB // 2
    groups = PEER_PICKS // SUBLANES
    lookahead = PEER_SLOTS - 1

    def slab(slot, j):
        return bufs[slot].at[pl.ds(j * PEER_SLAB_PITCH, PEER_SLAB)]

    def issue(tok, slot):
        for j in range(PEER_PICKS):
            pltpu.make_async_copy(uv_ref.at[ei_ref[j, tok]], slab(slot, j), sem_ref.at[slot]).start(priority=j % 2)

    def wait(slot):
        for j in range(PEER_PICKS):
            pltpu.make_async_copy(uv_ref.at[0], slab(slot, j), sem_ref.at[slot]).wait()

    gw_t = gw_ref[...]
    tok_lane = lax.broadcasted_iota(jnp.int32, gw_t.shape, 1)

    def compute(tok, slot):
        def words(j0, r):
            w = bufs[slot][pl.ds(j0 * PEER_SLAB_PITCH + r, SUBLANES, stride=PEER_SLAB_PITCH), :]
            return (lax.bitcast_convert_type(w << 16, F32),
                    lax.bitcast_convert_type(w & jnp.uint32(0xFFFF0000), F32))

        xs = [x_ref[tok, pl.ds(r, 1), :] for r in range(2 * half)]
        parts = []
        for jg in range(groups):
            acc = None
            for r in range(half):
                lo, hi = words(jg * SUBLANES, r)
                term = lo * xs[r] + hi * xs[half + r]
                acc = term if acc is None else acc + term
            parts.append(jnp.sum(acc, -1, keepdims=True))
        a = jnp.concatenate(parts, axis=0)
        gw_col = jnp.sum(jnp.where(tok_lane == tok, gw_t, 0.0), -1, keepdims=True)
        w = gw_col * jax.nn.gelu(a)
        ws = [w[jg * SUBLANES:(jg + 1) * SUBLANES] for jg in range(groups)]
        for r in range(half):
            acc_lo = acc_hi = None
            for jg in range(groups):
                lo, hi = words(jg * SUBLANES, half + r)
                acc_lo = lo * ws[jg] if acc_lo is None else acc_lo + lo * ws[jg]
                acc_hi = hi * ws[jg] if acc_hi is None else acc_hi + hi * ws[jg]
            o_ref[tok, pl.ds(r, 1), :] = jnp.sum(acc_lo, 0, keepdims=True)
            o_ref[tok, pl.ds(half + r, 1), :] = jnp.sum(acc_hi, 0, keepdims=True)

    def round_of_slots(base, n_issue):
        for s in range(PEER_SLOTS):
            wait(s)
            if s < n_issue:
                issue(base + s + lookahead, (s + lookahead) % PEER_SLOTS)
            compute(base + s, s)

    for s in range(lookahead):
        issue(s, s)
    n_rounds = n_tok // PEER_SLOTS

    def round_body(i, _):
        round_of_slots(i * PEER_SLOTS, PEER_SLOTS)
        return 0

    lax.fori_loop(0, n_rounds - 1, round_body, 0)
    round_of_slots((n_rounds - 1) * PEER_SLOTS, PEER_SLOTS - lookahead)


def _peer_experts(eidx_t, gw_t, x3, uv):
    t = x3.shape[0]
    tt = PEER_TOK_TILE
    return pl.pallas_call(
        _peer_expert_kernel,
        out_shape=jax.ShapeDtypeStruct((t, PEER_SLAB, LANES), F32),
        grid=(t // tt,),
        in_specs=[pl.BlockSpec((PEER_PICKS, tt), lambda i: (0, i), memory_space=pltpu.SMEM),
                  pl.BlockSpec((PEER_PICKS, tt), lambda i: (0, i)),
                  pl.BlockSpec((tt, PEER_SLAB, LANES), lambda i: (i, 0, 0)),
                  pl.BlockSpec(memory_space=pl.ANY)],
        out_specs=pl.BlockSpec((tt, PEER_SLAB, LANES), lambda i: (i, 0, 0)),
        scratch_shapes=[pltpu.VMEM((PEER_PICKS * PEER_SLAB_PITCH, LANES), jnp.uint32) for _ in range(PEER_SLOTS)]
                       + [pltpu.SemaphoreType.DMA((PEER_SLOTS,))],
        compiler_params=_cparams("arbitrary"),
        name="peer_experts",
    )(eidx_t, gw_t, x3, uv)


def _add_ln_kernel(h_ref, f_ref, g_ref, b_ref, o_ref):
    o_ref[...] = _layernorm(DN_ALPHA * h_ref[...] + f_ref[...], g_ref[...], b_ref[...])


def _add_ln(h, f, g, b, tm):
    t, d = h.shape
    row = pl.BlockSpec((tm, d), lambda i: (i, 0))
    const = pl.BlockSpec((1, d), lambda i: (0, 0))
    return pl.pallas_call(
        _add_ln_kernel,
        out_shape=jax.ShapeDtypeStruct((t, d), F32),
        grid=(t // tm,),
        in_specs=[row, row, const, const],
        out_specs=row,
        compiler_params=_cparams("parallel"),
        name="add_ln",
    )(h, f, g, b)


def _alibi_slopes():
    s = (2.0 ** (-8.0 * (np.arange(N_MIX_HEADS) + 1) / N_MIX_HEADS)).astype(np.float32)
    return s[0::2], s[1::2]


def _mixer(x2, batch, seq, w_in, pe_k, w1_k, w2_k, pe_v, w1_v, w2_v):
    hd = HEAD_DIM
    sizes = [NSA_HEADS * hd] + [NSA_KV_GROUPS * hd] * 6 + [NSA_HEADS * 3] + [MOBA_HEADS * hd] * 3
    offs = np.concatenate([[0], np.cumsum(sizes)])
    sec = {n: w_in[:, offs[i]:offs[i + 1]] for i, n in enumerate(
        ["nq", "kc", "vc", "ks", "vs", "kw", "vw", "gl", "mq", "mk", "mv"])}
    order = ["nq", "ks", "kw", "vw", "mq", "mk"]
    cols, at = {}, 0
    for n in order:
        cols[n] = at // LANES
        at += sec[n].shape[1]
    w_main = jnp.concatenate([sec[n] for n in order], axis=1).astype(MXU_DTYPE)
    scale = hd ** -0.5
    col_scale = jnp.concatenate([
        jnp.full((1, sec[n].shape[1]), scale if n in ("nq", "mq") else 1.0, F32) for n in order], axis=1)
    per_group = NSA_HPG * 3
    gl_cols = [jnp.pad(sec["gl"][:, g * per_group:(g + 1) * per_group], ((0, 0), (0, LANES - per_group)))
               for g in range(NSA_KV_GROUPS)]
    w_aux = jnp.concatenate([sec["kc"], sec["vc"]] + gl_cols, axis=1).astype(MXU_DTYPE)
    cols["gl_aux"] = (2 * NSA_KV_GROUPS * hd) // LANES

    xb = x2.astype(MXU_DTYPE)
    tm = 512 if x2.shape[0] % 512 == 0 else 256
    proj = _matmul(xb, w_main, col_scale, MXU_DTYPE, tm, 768, "in_proj")
    w_vt = jnp.concatenate([sec["vs"], sec["mv"]], axis=1).T.astype(MXU_DTYPE)
    v_t = _project_t(w_vt, xb, batch, seq, tm)
    cols["vs_t"], cols["mv_t"], cols["vt_heads"] = 0, NSA_KV_GROUPS, NSA_KV_GROUPS + MOBA_HEADS
    aux = _matmul(xb, w_aux, jnp.ones((1, w_aux.shape[1]), F32), F32, tm, 256, "in_proj_aux")

    ns = seq // NSA_CMP_STRIDE

    def strips(col0):
        raw = aux[:, col0:col0 + NSA_KV_GROUPS * hd].reshape(batch, seq, NSA_KV_GROUPS, hd)
        return raw.transpose(0, 2, 1, 3).reshape(batch * NSA_KV_GROUPS, ns, NSA_CMP_STRIDE * hd)

    kc = _compress(strips(0), pe_k, w1_k, w2_k)
    vc = _compress(strips(NSA_KV_GROUPS * hd), pe_v, w1_v, w2_v)

    slope_n, slope_m = _alibi_slopes()
    sn = np.repeat(slope_n.reshape(NSA_KV_GROUPS, NSA_HPG), Q_BLK, axis=1)
    sn_rows = jnp.asarray(np.broadcast_to(sn[:, :, None], sn.shape + (LANES,)).copy())
    sn_lanes = jnp.asarray(sn[:, None, :].copy())
    sm = jnp.asarray(np.broadcast_to(slope_m[:, None, None], (MOBA_HEADS, 1, LANES)).copy())

    key = lambda n: lax.broadcasted_iota(jnp.int32, (n, 1), 0)
    off_n = key(SEL_STEP) - (lax.broadcasted_iota(jnp.int32, (1, NSA_ROWS), 1) & (Q_BLK - 1))
    bias_n = sn_lanes * (-off_n).astype(F32)[None]
    off_m = key(MOBA_STEP) - lax.broadcasted_iota(jnp.int32, (1, MOBA_BLOCK), 1)
    bias_m = jnp.asarray(slope_m)[:, None, None] * (-off_m).astype(F32)[None]

    o_nsa = _nsa(proj, aux, kc, vc, v_t, sn_rows, sn_lanes, off_n, bias_n, batch, seq, cols)
    o_moba = _moba(proj, v_t, sm, off_m, bias_m, batch, seq, cols)
    return o_nsa, o_moba


def _memory_xattn_ln(h, mem2, batch, seq, wq, wkv, wo, g, b):
    mem_len = mem2.shape[0] // batch
    e2 = wkv.shape[1]
    kv = _matmul(mem2.astype(MXU_DTYPE), wkv.astype(MXU_DTYPE), jnp.ones((1, e2), F32), MXU_DTYPE,
                 mem_len, e2 // 2, "xattn_kv")
    return _xattn_ln(h, kv, wq.astype(MXU_DTYPE), wo.astype(MXU_DTYPE), g, b, batch, seq, mem_len, 256)


def _pack_tables_kernel(u_ref, v_ref, o_ref):
    def words(x):
        bits = lax.bitcast_convert_type(x.astype(jnp.bfloat16).astype(F32), jnp.uint32)
        half = x.shape[1] // 2
        return (bits[:, :half] >> 16) | (bits[:, half:] & jnp.uint32(0xFFFF0000))

    n = u_ref.shape[0]
    per_table = PEER_SLAB // 2
    for t, ref in enumerate((u_ref, v_ref)):
        w = words(ref[...])
        for r in range(per_table):
            o_ref[pl.ds(t * per_table + r, n, stride=PEER_SLAB), :] = w[:, r * LANES:(r + 1) * LANES]


def _pack_expert_tables(exp_u, exp_v):
    n_exp, d = exp_u.shape
    blk = 256
    packed = pl.pallas_call(
        _pack_tables_kernel,
        out_shape=jax.ShapeDtypeStruct((n_exp * PEER_SLAB, LANES), jnp.uint32),
        grid=(n_exp // blk,),
        in_specs=[pl.BlockSpec((blk, d), lambda i: (i, 0)), pl.BlockSpec((blk, d), lambda i: (i, 0))],
        out_specs=pl.BlockSpec((blk * PEER_SLAB, LANES), lambda i: (i, 0)),
        compiler_params=_cparams("parallel"),
        name="peer_pack_tables",
    )(exp_u, exp_v)
    return packed.reshape(n_exp, PEER_SLAB, LANES)


def _peer_ln(h, h_lowp, wq, sub_keys, exp_u, exp_v, g, b):
    t, d = h.shape
    gw_t, eidx_t = _peer_route(h_lowp, wq.astype(MXU_DTYPE), sub_keys.astype(MXU_DTYPE), 256)
    uv = _pack_expert_tables(exp_u, exp_v)
    f = _peer_experts(eidx_t, gw_t, h.reshape(t, PEER_SLAB, LANES), uv).reshape(t, d)
    return _add_ln(h, f, g, b, 256)


def kernel(x, mem, w_in, cmp_pe_k, cmp_w1_k, cmp_w2_k, cmp_pe_v, cmp_w1_v, cmp_w2_v, w_out, ln1_g, ln1_b,
           xa_wq, xa_wkv, xa_wo, ln2_g, ln2_b, peer_wq, peer_subkeys, peer_u, peer_v, ln3_g, ln3_b):
    batch, seq, d = x.shape
    assert seq % MOBA_STEP == 0 and WIN_SPAN <= seq <= NSA_SEL_BLOCK * LANES
    assert d == PEER_SLAB * LANES and w_in.shape[0] == DEPTH and PEER_TOPK == 2 * SUBLANES
    row = lambda v: v.reshape(1, d)
    h = x.reshape(batch * seq, d)
    mem2 = mem.reshape(-1, d)
    for l in range(DEPTH):
        o_nsa, o_moba = _mixer(h, batch, seq, w_in[l], cmp_pe_k[l], cmp_w1_k[l], cmp_w2_k[l],
                               cmp_pe_v[l], cmp_w1_v[l], cmp_w2_v[l])
        wo = w_out[l].astype(MXU_DTYPE)
        k1 = o_nsa.shape[1]
        h = _outproj_ln(o_nsa, o_moba, h, wo[:k1], wo[k1:], row(ln1_g[l]), row(ln1_b[l]), 256)
        h, h_lowp = _memory_xattn_ln(h, mem2, batch, seq, xa_wq[l], xa_wkv[l], xa_wo[l],
                                     row(ln2_g[l]), row(ln2_b[l]))
        h = _peer_ln(h, h_lowp, peer_wq[l], peer_subkeys[l], peer_u[l], peer_v[l], row(ln3_g[l]), row(ln3_b[l]))
    return h.reshape(batch, seq, d)
```
